```python
import math
import jax, jax.numpy as jnp
from jax import lax
import numpy as np

D_MODEL = 1024
BATCH = 8
SEQ = 4096
DEPTH = 1

HEAD_DIM = 64
N_HEADS = D_MODEL // HEAD_DIM
N_HEADS_A = N_HEADS // 2
N_HEADS_B = N_HEADS - N_HEADS_A
WIDTH_A = N_HEADS_A * HEAD_DIM
WIDTH_B = N_HEADS_B * HEAD_DIM
DILATED_PAIRS = ((128, 1), (512, 4), (2048, 16))
MOBA_BLOCK = 256
MOBA_TOPK = 3
MOBA_QUERY_CHUNK = 16
D_FF = 4 * D_MODEL
PLE_DIM = 256
EPS = 1e-6

kernel_name = 'hybrid_dilated_moba_block'


def alibi_slopes(n):
    return jnp.asarray(np.array([2.0 ** (-8.0 * (i + 1) / n) for i in range(n)], dtype=np.float32))


def rmsnorm(x, g):
    xf = x.astype(jnp.float32)
    y = xf * lax.rsqrt(jnp.mean(xf * xf, axis=-1, keepdims=True) + EPS)
    return (y * g.astype(jnp.float32)).astype(x.dtype)


def dilated_branch(q, k, v, slopes, window, dilation):
    b, h, s, hd = q.shape
    nk = window // dilation
    c = nk
    span = dilation * c
    s_pad = -(-s // span) * span
    L = s_pad // dilation
    nc = L // c

    def to_sub(t):
        t = jnp.pad(t, ((0, 0), (0, 0), (0, s_pad - s), (0, 0))).reshape(b, h, L, dilation, hd)
        return jnp.swapaxes(t, 2, 3).reshape(b, h, dilation, nc, c, hd)

    def with_prev(t):
        prev = jnp.pad(t[:, :, :, :-1], ((0, 0), (0, 0), (0, 0), (1, 0), (0, 0), (0, 0)))
        return jnp.concatenate([prev, t], axis=4)

    qs, ks, vs = to_sub(q), to_sub(k), to_sub(v)
    kk, vv = with_prev(ks), with_prev(vs)
    scores = jnp.einsum('bhrnqd,bhrnkd->bhrnqk', qs, kk,
                        preferred_element_type=jnp.float32) * (hd ** -0.5)
    qi = jnp.arange(c)[:, None]
    ki = jnp.arange(2 * c)[None, :]
    m = c + qi - ki
    key_sub = jnp.arange(nc)[:, None, None] * c + ki[None] - c
    valid = (m >= 0) & (m <= nk) & (key_sub >= 0)
    bias = -slopes[None, :, None, None, None, None] * (m * dilation).astype(jnp.float32)
    scores = jnp.where(valid, scores + bias, -jnp.inf)
    lse = jax.nn.logsumexp(scores, axis=-1)
    probs = jnp.exp(scores - lse[..., None])
    out = jnp.einsum('bhrnqk,bhrnkd->bhrnqd', probs, vv, preferred_element_type=jnp.float32)

    def from_sub(t):
        rest = t.shape[5:]
        t = jnp.swapaxes(t.reshape((b, h, dilation, L) + rest), 2, 3)
        return t.reshape((b, h, s_pad) + rest)[:, :, :s]

    return from_sub(out), from_sub(lse)


def dilated_mixture(q, k, v, slopes):
    outs, lses = [], []
    for window, dilation in DILATED_PAIRS:
        o, l = dilated_branch(q, k, v, slopes, window, dilation)
        outs.append(o)
        lses.append(l)
    w = jax.nn.softmax(jnp.stack(lses, 0), axis=0)
    return jnp.sum(w[..., None] * jnp.stack(outs, 0), axis=0)


def moba_attention(q, k, v, slopes):
    b, h, s, hd = q.shape
    bs = MOBA_BLOCK
    s_pad = -(-s // bs) * bs
    nblk = s_pad // bs
    pad = ((0, 0), (0, 0), (0, s_pad - s), (0, 0))
    q, k, v = jnp.pad(q, pad), jnp.pad(k, pad), jnp.pad(v, pad)
    kb = k.reshape(b, h, nblk, bs, hd)
    vb = v.reshape(b, h, nblk, bs, hd)
    kmean = jnp.mean(kb.astype(jnp.float32), axis=3)
    pos = jnp.arange(s_pad)
    own = pos // bs
    gate = jnp.einsum('bhsd,bhnd->bhsn', q.astype(jnp.float32), kmean)
    past = jnp.arange(nblk)[None, :] < own[:, None]
    gate = jnp.where(past, gate, -jnp.inf)
    ksel = min(MOBA_TOPK, nblk)
    _, top_idx = lax.top_k(gate, ksel)
    top_valid = top_idx < own[:, None]
    own_b = jnp.broadcast_to(own[None, None, :, None], (b, h, s_pad, 1)).astype(top_idx.dtype)
    sel = jnp.concatenate([top_idx, own_b], axis=-1)
    sel_valid = jnp.concatenate([top_valid, jnp.ones((b, h, s_pad, 1), dtype=bool)], axis=-1)
    nsel = ksel + 1
    qc = MOBA_QUERY_CHUNK
    nq = s_pad // qc
    scale = hd ** -0.5
    bi = jnp.arange(b)[:, None, None, None]
    hi = jnp.arange(h)[None, :, None, None]

    def chunkify(t):
        t = t.reshape((b, h, nq, qc) + t.shape[3:])
        return jnp.moveaxis(t, 2, 0)

    def step(args):
        q_c, sel_c, val_c, pos_c = args
        kg = kb[bi, hi, sel_c]
        vg = vb[bi, hi, sel_c]
        sc = jnp.einsum('bhqd,bhqjkd->bhqjk', q_c, kg,
                        preferred_element_type=jnp.float32) * scale
        key_pos = sel_c[..., None] * bs + jnp.arange(bs)
        dist = pos_c[None, None, :, None, None] - key_pos
        mask = val_c[..., None] & (dist >= 0)
        sc = jnp.where(mask, sc - slopes[None, :, None, None, None] * dist.astype(jnp.float32), -jnp.inf)
        pr = jax.nn.softmax(sc.reshape(b, h, qc, nsel * bs), axis=-1).reshape(b, h, qc, nsel, bs)
        return jnp.einsum('bhqjk,bhqjkd->bhqd', pr, vg, preferred_element_type=jnp.float32)

    out = lax.map(step, (chunkify(q), chunkify(sel), chunkify(sel_valid), pos.reshape(nq, qc)))
    out = jnp.moveaxis(out, 0, 2).reshape(b, h, s_pad, hd)
    return out[:, :, :s]


def split_heads(t, nh):
    b, s, _ = t.shape
    t = t.reshape(b, s, 3, nh, HEAD_DIM).transpose(2, 0, 3, 1, 4)
    return t[0], t[1], t[2]


def merge_heads(t):
    b, h, s, hd = t.shape
    return t.transpose(0, 2, 1, 3).reshape(b, s, h * hd)


def setup_inputs(seed: int = 0) -> dict:
    key = jax.random.key(seed)
    ks = jax.random.split(key, 16)
    f32 = jnp.float32

    def nrm(k, shape, fan_in):
        return jax.random.normal(k, shape, f32) * fan_in ** -0.5

    def gain(k, shape):
        return 1.0 + 0.02 * jax.random.normal(k, shape, f32)

    return {
        'x': jax.random.normal(ks[0], (BATCH, SEQ, D_MODEL), f32),
        'p': jax.random.normal(ks[1], (DEPTH, BATCH, SEQ, PLE_DIM), f32),
        'g_attn': gain(ks[2], (DEPTH, D_MODEL)),
        'w_in': nrm(ks[3], (DEPTH, D_MODEL, 3 * D_MODEL), D_MODEL),
        'g_out_a': gain(ks[4], (DEPTH, WIDTH_A)),
        'g_out_b': gain(ks[5], (DEPTH, WIDTH_B)),
        'w_out': nrm(ks[6], (DEPTH, D_MODEL, D_MODEL), D_MODEL),
        'g_mlp': gain(ks[7], (DEPTH, D_MODEL)),
        'w_up': nrm(ks[8], (DEPTH, D_MODEL, D_FF), D_MODEL),
        'w_down': nrm(ks[9], (DEPTH, D_FF, D_MODEL), D_FF),
        'g_ple': gain(ks[10], (DEPTH, D_MODEL)),
        'w_ple_gate': nrm(ks[11], (DEPTH, D_MODEL, D_MODEL), D_MODEL),
        'b_ple_gate': 0.02 * jax.random.normal(ks[12], (DEPTH, D_MODEL), f32),
        'w_ple_proj': nrm(ks[13], (DEPTH, PLE_DIM, D_MODEL), PLE_DIM),
        'g_final': gain(ks[14], (D_MODEL,)),
    }


def reference(x, p, g_attn, w_in, g_out_a, g_out_b, w_out, g_mlp, w_up, w_down,
              g_ple, w_ple_gate, b_ple_gate, w_ple_proj, g_final):
    slopes = alibi_slopes(N_HEADS)
    slopes_a = slopes[0::2]
    slopes_b = slopes[1::2]
    h = x
    for i in range(DEPTH):
        hn = rmsnorm(h, g_attn[i])
        qkv = hn @ w_in[i]
        qa, ka, va = split_heads(qkv[..., :3 * WIDTH_A], N_HEADS_A)
        qb, kb, vb = split_heads(qkv[..., 3 * WIDTH_A:], N_HEADS_B)
        ya = rmsnorm(merge_heads(dilated_mixture(qa, ka, va, slopes_a)), g_out_a[i])
        yb = rmsnorm(merge_heads(moba_attention(qb, kb, vb, slopes_b)), g_out_b[i])
        y = jnp.concatenate([ya, yb], axis=-1).astype(h.dtype) @ w_out[i]
        h = h + y
        hn = rmsnorm(h, g_mlp[i])
        h = h + jnp.square(jax.nn.relu(hn @ w_up[i])) @ w_down[i]
        gate = jax.nn.sigmoid(rmsnorm(h, g_ple[i]) @ w_ple_gate[i] + b_ple_gate[i])
        h = h + gate * (p[i] @ w_ple_proj[i])
    return rmsnorm(h, g_final)
```

```python
import functools
import math

import jax
import jax.numpy as jnp
import numpy as np
from jax import lax
from jax.experimental import pallas as pl
from jax.experimental.pallas import tpu as pltpu

D_MODEL = 1024
HEAD_DIM = 64
N_HEADS = D_MODEL // HEAD_DIM
N_HEADS_A = N_HEADS // 2
N_HEADS_B = N_HEADS - N_HEADS_A
WIDTH_A = N_HEADS_A * HEAD_DIM
WIDTH_B = N_HEADS_B * HEAD_DIM
DILATED_PAIRS = ((128, 1), (512, 4), (2048, 16))
MOBA_BLOCK = 256
MOBA_TOPK = 3
D_FF = 4 * D_MODEL
PLE_DIM = 256
EPS = 1e-6

LANES = 128
LOG2E = 1.4426950408889634
Q_SCALE = HEAD_DIM ** -0.5 * LOG2E
NEG_INF = float("-inf")
VMEM_LIMIT = 56 * 1024 * 1024

PROJ_ROWS = 512
TAIL_ROWS = 512
DIL_BLOCK = 128


def _alibi_log2_slopes():
    s = np.array([2.0 ** (-8.0 * (i + 1) / N_HEADS) for i in range(N_HEADS)], dtype=np.float64)
    return (s[0::2] * LOG2E).astype(np.float32), (s[1::2] * LOG2E).astype(np.float32)


def _dot_nt(a, b):
    return lax.dot_general(a, b, (((1,), (1,)), ((), ())), preferred_element_type=jnp.float32)


def _split_bf16(x):
    hi = x.astype(jnp.bfloat16)
    lo = (x - hi.astype(jnp.float32)).astype(jnp.bfloat16)
    return hi, lo


def _proj_kernel(x_ref, g_ref, w_ref, qkv_ref, gate_ref, kmean_ref):
    t = pl.program_id(1)
    x = x_ref[0]
    ms = jnp.mean(x * x, axis=-1, keepdims=True)
    hn = (x * lax.rsqrt(ms + EPS)) * g_ref[...]
    qkv = jnp.dot(hn.astype(jnp.bfloat16), w_ref[...], preferred_element_type=jnp.float32)

    qa = qkv[:, 0:WIDTH_A]
    qb = qkv[:, 3 * WIDTH_A:3 * WIDTH_A + WIDTH_B]
    kb = qkv[:, 3 * WIDTH_A + WIDTH_B:3 * WIDTH_A + 2 * WIDTH_B]
    qkv_ref[0, :, 0:WIDTH_A] = (qa * Q_SCALE).astype(qkv_ref.dtype)
    qkv_ref[0, :, WIDTH_A:3 * WIDTH_A] = qkv[:, WIDTH_A:3 * WIDTH_A].astype(qkv_ref.dtype)
    qkv_ref[0, :, 3 * WIDTH_A:3 * WIDTH_A + WIDTH_B] = (qb * Q_SCALE).astype(qkv_ref.dtype)
    qkv_ref[0, :, 3 * WIDTH_A + WIDTH_B:] = qkv[:, 3 * WIDTH_A + WIDTH_B:].astype(qkv_ref.dtype)

    @pl.when(t == 0)
    def _():
        kmean_ref[...] = jnp.zeros_like(kmean_ref)

    blocks_per_tile = PROJ_ROWS // MOBA_BLOCK
    for sb in range(blocks_per_tile):
        km = jnp.mean(kb[sb * MOBA_BLOCK:(sb + 1) * MOBA_BLOCK, :], axis=0, keepdims=True)
        kmean_ref[pl.ds(t * blocks_per_tile + sb, 1), :] = km

    kmean = kmean_ref[...]
    lane_head = lax.broadcasted_iota(jnp.int32, kmean.shape, 1) // HEAD_DIM
    wt = jnp.concatenate(
        [jnp.where(lane_head == h, kmean, 0.0) for h in range(N_HEADS_B)], axis=0)
    q_hi, q_lo = _split_bf16(qb)
    w_hi, w_lo = _split_bf16(wt)
    gate_ref[0] = _dot_nt(q_hi, w_hi) + (_dot_nt(q_hi, w_lo) + _dot_nt(q_lo, w_hi))


def _project(x, g_attn, w_in_bf16):
    b, s, d = x.shape
    n_blocks = s // MOBA_BLOCK
    return pl.pallas_call(
        _proj_kernel,
        grid=(b, s // PROJ_ROWS),
        in_specs=[
            pl.BlockSpec((1, PROJ_ROWS, d), lambda i, t: (i, t, 0)),
            pl.BlockSpec((1, d), lambda i, t: (0, 0)),
            pl.BlockSpec((d, 3 * d), lambda i, t: (0, 0)),
        ],
        out_specs=[
            pl.BlockSpec((1, PROJ_ROWS, 3 * d), lambda i, t: (i, t, 0)),
            pl.BlockSpec((1, PROJ_ROWS, LANES), lambda i, t: (i, t, 0)),
        ],
        out_shape=[
            jax.ShapeDtypeStruct((b, s, 3 * d), jnp.bfloat16),
            jax.ShapeDtypeStruct((b, s, LANES), jnp.float32),
        ],
        scratch_shapes=[pltpu.VMEM((n_blocks, WIDTH_B), jnp.float32)],
        compiler_params=pltpu.CompilerParams(
            dimension_semantics=("arbitrary", "arbitrary"), vmem_limit_bytes=VMEM_LIMIT),
        name="proj",
    )(x, g_attn, w_in_bf16)


def _select_kernel(gate_ref, sel_ref):
    own = pl.program_id(1)
    n_blocks = LANES // N_HEADS_B
    g = gate_ref[0]
    blk = lax.broadcasted_iota(jnp.int32, g.shape, 1) % n_blocks
    past = blk < own
    g = jnp.where(past, g, NEG_INF)
    rank = jnp.zeros(g.shape, jnp.int32)
    for sft in range(1, n_blocks):
        wraps = blk + sft >= n_blocks
        other = jnp.where(wraps,
                          pltpu.roll(g, n_blocks - sft, axis=1),
                          pltpu.roll(g, LANES - sft, axis=1))
        beats = (other > g) | ((other == g) & wraps)
        rank = rank + beats.astype(jnp.int32)
    sel_ref[0] = (past & (rank < MOBA_TOPK)).astype(sel_ref.dtype)


def _select(gate):
    b, s, _ = gate.shape
    return pl.pallas_call(
        _select_kernel,
        grid=(b, s // MOBA_BLOCK),
        in_specs=[pl.BlockSpec((1, MOBA_BLOCK, LANES), lambda i, t: (i, t, 0))],
        out_specs=pl.BlockSpec((1, MOBA_BLOCK, LANES), lambda i, t: (i, t, 0)),
        out_shape=jax.ShapeDtypeStruct((b, s, LANES), jnp.bfloat16),
        compiler_params=pltpu.CompilerParams(dimension_semantics=("arbitrary", "arbitrary")),
        name="select",
    )(gate)


def _dilated_kernel(slopes_ref, q1_ref, k1_ref, v1_ref, q4_ref, k4_ref, v4_ref,
                    q16_ref, k16_ref, v16_ref, o_ref, acc_ref, m_ref, l_ref):
    pair = pl.program_id(1)
    seq = acc_ref.shape[0]
    c = DIL_BLOCK
    acc_ref[...] = jnp.zeros_like(acc_ref)
    m_ref[...] = jnp.full_like(m_ref, NEG_INF)
    l_ref[...] = jnp.zeros_like(l_ref)

    lane = lax.broadcasted_iota(jnp.int32, (c, LANES), 1)
    head0 = lane < HEAD_DIM
    rel = (lax.broadcasted_iota(jnp.int32, (c, 2 * c), 0)
           - lax.broadcasted_iota(jnp.int32, (c, 2 * c), 1))

    branches = ((1, q1_ref, k1_ref, v1_ref), (4, q4_ref, k4_ref, v4_ref), (16, q16_ref, k16_ref, v16_ref))
    for dil, q_ref, k_ref, v_ref in branches:
        n_chunks = seq // (dil * c)

        def block_step(idx, carry, dil=dil, q_ref=q_ref, k_ref=k_ref, v_ref=v_ref, n_chunks=n_chunks):
            r = idx // n_chunks
            n = idx % n_chunks
            q0 = pl.multiple_of(n * c, c)
            k0 = pl.multiple_of(jnp.maximum(n - 1, 0) * c, c)
            dist = rel + (q0 - k0)
            valid = (dist >= 0) & (dist <= c)
            distf = dist.astype(jnp.float32)
            q = q_ref[0, r, pl.ds(q0, c), :]
            k = k_ref[0, r, pl.ds(k0, 2 * c), :]
            v = v_ref[0, r, pl.ds(k0, 2 * c), :]
            if dil == 1:
                rows = pl.ds(q0, c)
            else:
                rows = pl.ds(r + dil * q0, c, stride=dil)
            alphas, pvs = [], []
            for h in range(2):
                slope = slopes_ref[2 * pair + h] * dil
                qh = jnp.where(head0 if h == 0 else ~head0, q, jnp.zeros_like(q))
                s = _dot_nt(qh, k)
                s = jnp.where(valid, s - slope * distf, NEG_INF)
                m_old = m_ref[h, rows, :][:, 0:1]
                l_old = l_ref[h, rows, :][:, 0:1]
                m_new = jnp.maximum(m_old, jnp.max(s, axis=-1, keepdims=True))
                p = jnp.exp2(s - m_new)
                alpha = jnp.exp2(m_old - m_new)
                l_new = alpha * l_old + jnp.sum(p, axis=-1, keepdims=True)
                m_ref[h, rows, :] = jnp.broadcast_to(m_new, (c, LANES))
                l_ref[h, rows, :] = jnp.broadcast_to(l_new, (c, LANES))
                alphas.append(alpha)
                pvs.append(jnp.dot(p.astype(v.dtype), v, preferred_element_type=jnp.float32))
            alpha_l = jnp.where(head0, alphas[0], alphas[1])
            pv_l = jnp.where(head0, pvs[0], pvs[1])
            acc_ref[rows, :] = acc_ref[rows, :] * alpha_l + pv_l
            return carry

        lax.fori_loop(0, dil * n_chunks, block_step, 0)

    lane_s = lax.broadcasted_iota(jnp.int32, (seq, LANES), 1)
    l_all = jnp.where(lane_s < HEAD_DIM, l_ref[0], l_ref[1])
    o_ref[0] = (acc_ref[...] / l_all).astype(o_ref.dtype)


def _dilated(qkv, qkv4, qkv16, slopes):
    b, s, _ = qkv.shape
    n_pairs = WIDTH_A // LANES
    col = lambda part: (lambda i, p: (i, 0, part * n_pairs + p))
    colp = lambda part: (lambda i, p: (i, 0, 0, part * n_pairs + p))
    in_specs = [pl.BlockSpec(memory_space=pltpu.SMEM)]
    in_specs += [pl.BlockSpec((1, 1, s, LANES), colp(part)) for part in range(3)]
    in_specs += [pl.BlockSpec((1, 4, s // 4, LANES), colp(part)) for part in range(3)]
    in_specs += [pl.BlockSpec((1, 16, s // 16, LANES), colp(part)) for part in range(3)]
    qkv1 = qkv.reshape(b, 1, s, qkv.shape[-1])
    return pl.pallas_call(
        _dilated_kernel,
        grid=(b, n_pairs),
        in_specs=in_specs,
        out_specs=pl.BlockSpec((1, s, LANES), lambda i, p: (i, 0, p)),
        out_shape=jax.ShapeDtypeStruct((b, s, WIDTH_A), jnp.float32),
        scratch_shapes=[pltpu.VMEM((s, LANES), jnp.float32),
                        pltpu.VMEM((2, s, LANES), jnp.float32),
                        pltpu.VMEM((2, s, LANES), jnp.float32)],
        compiler_params=pltpu.CompilerParams(
            dimension_semantics=("arbitrary", "arbitrary"), vmem_limit_bytes=VMEM_LIMIT),
        name="dilated",
    )(slopes, qkv1, qkv1, qkv1, qkv4, qkv4, qkv4, qkv16, qkv16, qkv16)


def _moba_kernel(slopes_ref, q_ref, k_ref, v_ref, sel_ref, o_ref):
    pair = pl.program_id(1)
    i = pl.program_id(2)
    bs = MOBA_BLOCK
    q = q_ref[0]
    sel = sel_ref[0]
    lane = lax.broadcasted_iota(jnp.int32, (bs, LANES), 1)
    head0 = lane < HEAD_DIM
    qh = (jnp.where(head0, q, jnp.zeros_like(q)), jnp.where(head0, jnp.zeros_like(q), q))
    slopes = (slopes_ref[2 * pair], slopes_ref[2 * pair + 1])
    rel = (lax.broadcasted_iota(jnp.int32, (bs, bs), 0)
           - lax.broadcasted_iota(jnp.int32, (bs, bs), 1))
    relf = rel.astype(jnp.float32)

    d0 = pl.multiple_of(i * bs, bs)
    kd = k_ref[0, pl.ds(d0, bs), :]
    vd = v_ref[0, pl.ds(d0, bs), :]
    state = []
    for h in range(2):
        s = _dot_nt(qh[h], kd)
        s = jnp.where(rel >= 0, s - slopes[h] * relf, NEG_INF)
        m = jnp.max(s, axis=-1, keepdims=True)
        p = jnp.exp2(s - m)
        l = jnp.sum(p, axis=-1, keepdims=True)
        acc = jnp.dot(p.astype(vd.dtype), vd, preferred_element_type=jnp.float32)
        state += [m, l, acc]

    e_row = lax.broadcasted_iota(jnp.int32, (LANES, LANES), 0)
    e_col_head = (lax.broadcasted_iota(jnp.int32, (LANES, LANES), 1) >= HEAD_DIM).astype(jnp.int32)
    n_blocks = LANES // N_HEADS_B

    def past_block(j, carry):
        k0 = pl.multiple_of(j * bs, bs)
        kj = k_ref[0, pl.ds(k0, bs), :]
        vj = v_ref[0, pl.ds(k0, bs), :]
        onehot = (e_row == n_blocks * (2 * pair + e_col_head) + j).astype(sel.dtype)
        selw = jnp.dot(sel, onehot, preferred_element_type=jnp.float32)
        gap = ((i - j) * bs).astype(jnp.float32)
        out = []
        for h in range(2):
            m_old, l_old, acc_old = carry[3 * h:3 * h + 3]
            chosen = selw[:, h * HEAD_DIM:h * HEAD_DIM + 1] > 0.5
            s = _dot_nt(qh[h], kj) - slopes[h] * (relf + gap)
            m_blk = jnp.where(chosen, jnp.max(s, axis=-1, keepdims=True), NEG_INF)
            m_new = jnp.maximum(m_old, m_blk)
            p = jnp.exp2(s - m_new)
            alpha = jnp.exp2(m_old - m_new)
            l_new = alpha * l_old + jnp.where(chosen, jnp.sum(p, axis=-1, keepdims=True), 0.0)
            pv = jnp.dot(p.astype(vj.dtype), vj, preferred_element_type=jnp.float32)
            acc_new = alpha * acc_old + jnp.where(chosen, pv, 0.0)
            out += [m_new, l_new, acc_new]
        return tuple(out)

    m0, l0, acc0, m1, l1, acc1 = lax.fori_loop(0, i, past_block, tuple(state))
    o_ref[0] = jnp.where(head0, acc0 / l0, acc1 / l1).astype(o_ref.dtype)


def _moba(qkv, sel, slopes):
    b, s, _ = qkv.shape
    n_pairs = WIDTH_B // LANES
    base = 3 * WIDTH_A // LANES
    return pl.pallas_call(
        _moba_kernel,
        grid=(b, n_pairs, s // MOBA_BLOCK),
        in_specs=[
            pl.BlockSpec(memory_space=pltpu.SMEM),
            pl.BlockSpec((1, MOBA_BLOCK, LANES), lambda i, p, t: (i, t, base + p)),
            pl.BlockSpec((1, s, LANES), lambda i, p, t: (i, 0, base + n_pairs + p)),
            pl.BlockSpec((1, s, LANES), lambda i, p, t: (i, 0, base + 2 * n_pairs + p)),
            pl.BlockSpec((1, MOBA_BLOCK, LANES), lambda i, p, t: (i, t, 0)),
        ],
        out_specs=pl.BlockSpec((1, MOBA_BLOCK, LANES), lambda i, p, t: (i, t, p)),
        out_shape=jax.ShapeDtypeStruct((b, s, WIDTH_B), jnp.float32),
        compiler_params=pltpu.CompilerParams(
            dimension_semantics=("arbitrary", "arbitrary", "arbitrary"), vmem_limit_bytes=VMEM_LIMIT),
        name="moba",
    )(slopes, qkv, qkv, qkv, sel)


def _rms(x, g):
    return (x * lax.rsqrt(jnp.mean(x * x, axis=-1, keepdims=True) + EPS)) * g


def _tail_kernel(x_ref, p_ref, oa_ref, ob_ref, ga_ref, gb_ref, wout_ref, gmlp_ref, wup_ref, wdown_ref,
                 gple_ref, wgate_ref, bgate_ref, wproj_ref, gfin_ref, o_ref):
    bf = jnp.bfloat16
    ya = _rms(oa_ref[0].astype(jnp.float32), ga_ref[...]).astype(bf)
    yb = _rms(ob_ref[0].astype(jnp.float32), gb_ref[...]).astype(bf)
    y = (jnp.dot(ya, wout_ref[0:WIDTH_A, :], preferred_element_type=jnp.float32)
         + jnp.dot(yb, wout_ref[WIDTH_A:, :], preferred_element_type=jnp.float32))
    h = x_ref[0] + y

    hn = _rms(h, gmlp_ref[...]).astype(bf)
    ff_chunk = 1024
    for c in range(D_FF // ff_chunk):
        u = jnp.dot(hn, wup_ref[:, c * ff_chunk:(c + 1) * ff_chunk], preferred_element_type=jnp.float32)
        u = jnp.square(jnp.maximum(u, 0.0)).astype(bf)
        h = h + jnp.dot(u, wdown_ref[c * ff_chunk:(c + 1) * ff_chunk, :], preferred_element_type=jnp.float32)

    z = jnp.dot(_rms(h, gple_ref[...]).astype(bf), wgate_ref[...], preferred_element_type=jnp.float32)
    gate = jax.nn.sigmoid(z + bgate_ref[...])
    h = h + gate * jnp.dot(p_ref[0].astype(bf), wproj_ref[...], preferred_element_type=jnp.float32)
    o_ref[0] = _rms(h, gfin_ref[...]).astype(o_ref.dtype)


def _tail(x, p, oa, ob, ga, gb, wout, gmlp, wup, wdown, gple, wgate, bgate, wproj, gfin):
    b, s, d = x.shape
    rows = lambda width: pl.BlockSpec((1, TAIL_ROWS, width), lambda i, t: (i, t, 0))
    full = lambda arr: pl.BlockSpec(arr.shape, lambda i, t: (0,) * arr.ndim,
                                    pipeline_mode=pl.Buffered(1))
    consts = (ga, gb, wout, gmlp, wup, wdown, gple, wgate, bgate, wproj, gfin)
    return pl.pallas_call(
        _tail_kernel,
        grid=(b, s // TAIL_ROWS),
        in_specs=[rows(d), rows(PLE_DIM), rows(WIDTH_A), rows(WIDTH_B)] + [full(a) for a in consts],
        out_specs=rows(d),
        out_shape=jax.ShapeDtypeStruct((b, s, d), x.dtype),
        compiler_params=pltpu.CompilerParams(
            dimension_semantics=("arbitrary", "arbitrary"), vmem_limit_bytes=VMEM_LIMIT),
        name="tail",
    )(x, p, oa, ob, *consts)


def kernel(x, p, g_attn, w_in, g_out_a, g_out_b, w_out, g_mlp, w_up, w_down, g_ple, w_ple_gate,
           b_ple_gate, w_ple_proj, g_final):
    assert x.shape[1:] == (4096, D_MODEL) and p.shape[0] == 1 and w_in.shape[0] == 1
    b, s, _ = x.shape
    bf = jnp.bfloat16
    slopes_a, slopes_b = _alibi_log2_slopes()
    row = lambda v: v.reshape(1, -1)

    qkv, gate = _project(x, row(g_attn[0]), w_in[0].astype(bf))
    sel = _select(gate)

    qkv_a = qkv[..., :3 * WIDTH_A]
    perm = lambda dil: qkv_a.reshape(b, s // dil, dil, 3 * WIDTH_A).transpose(0, 2, 1, 3)
    oa = _dilated(qkv, perm(4), perm(16), jnp.asarray(slopes_a))
    ob = _moba(qkv, sel, jnp.asarray(slopes_b))

    return _tail(x, p[0], oa, ob, row(g_out_a[0]), row(g_out_b[0]), w_out[0].astype(bf), row(g_mlp[0]),
                 w_up[0].astype(bf), w_down[0].astype(bf), row(g_ple[0]), w_ple_gate[0].astype(bf),
                 row(b_ple_gate[0]), w_ple_proj[0].astype(bf), row(g_final))
```

```python
import functools
import math

import jax
import jax.numpy as jnp
import numpy as np
from jax import lax
from jax.experimental import pallas as pl
from jax.experimental.pallas import tpu as pltpu

D_MODEL = 1024
HEAD_DIM = 64
N_HEADS = D_MODEL // HEAD_DIM
N_HEADS_A = N_HEADS // 2
N_HEADS_B = N_HEADS - N_HEADS_A
WIDTH_A = N_HEADS_A * HEAD_DIM
WIDTH_B = N_HEADS_B * HEAD_DIM
DILATED_PAIRS = ((128, 1), (512, 4), (2048, 16))
MOBA_BLOCK = 256
MOBA_TOPK = 3
D_FF = 4 * D_MODEL
PLE_DIM = 256
EPS = 1e-6

LANES = 128
LOG2E = 1.4426950408889634
Q_SCALE = HEAD_DIM ** -0.5 * LOG2E
NEG_INF = float("-inf")
VMEM_LIMIT = 56 * 1024 * 1024

PROJ_ROWS = 512
TAIL_ROWS = 512
DIL_BLOCK = 128


def _alibi_log2_slopes():
    s = np.array([2.0 ** (-8.0 * (i + 1) / N_HEADS) for i in range(N_HEADS)], dtype=np.float64)
    return (s[0::2] * LOG2E).astype(np.float32), (s[1::2] * LOG2E).astype(np.float32)


def _dot_nt(a, b):
    return lax.dot_general(a, b, (((1,), (1,)), ((), ())), preferred_element_type=jnp.float32)


def _split_bf16(x):
    hi = x.astype(jnp.bfloat16)
    lo = (x - hi.astype(jnp.float32)).astype(jnp.bfloat16)
    return hi, lo


def _proj_kernel(x_ref, g_ref, w_ref, qkv_ref, gate_ref, kmean_ref):
    t = pl.program_id(1)
    x = x_ref[0]
    ms = jnp.mean(x * x, axis=-1, keepdims=True)
    hn = (x * lax.rsqrt(ms + EPS)) * g_ref[...]
    qkv = jnp.dot(hn.astype(jnp.bfloat16), w_ref[...], preferred_element_type=jnp.float32)

    qa = qkv[:, 0:WIDTH_A]
    qb = qkv[:, 3 * WIDTH_A:3 * WIDTH_A + WIDTH_B]
    kb = qkv[:, 3 * WIDTH_A + WIDTH_B:3 * WIDTH_A + 2 * WIDTH_B]
    qkv_ref[0, :, 0:WIDTH_A] = (qa * Q_SCALE).astype(qkv_ref.dtype)
    qkv_ref[0, :, WIDTH_A:3 * WIDTH_A] = qkv[:, WIDTH_A:3 * WIDTH_A].astype(qkv_ref.dtype)
    qkv_ref[0, :, 3 * WIDTH_A:3 * WIDTH_A + WIDTH_B] = (qb * Q_SCALE).astype(qkv_ref.dtype)
    qkv_ref[0, :, 3 * WIDTH_A + WIDTH_B:] = qkv[:, 3 * WIDTH_A + WIDTH_B:].astype(qkv_ref.dtype)

    @pl.when(t == 0)
    def _():
        kmean_ref[...] = jnp.zeros_like(kmean_ref)

    blocks_per_tile = PROJ_ROWS // MOBA_BLOCK
    for sb in range(blocks_per_tile):
        km = jnp.mean(kb[sb * MOBA_BLOCK:(sb + 1) * MOBA_BLOCK, :], axis=0, keepdims=True)
        kmean_ref[pl.ds(t * blocks_per_tile + sb, 1), :] = km

    kmean = kmean_ref[...]
    lane_head = lax.broadcasted_iota(jnp.int32, kmean.shape, 1) // HEAD_DIM
    wt = jnp.concatenate(
        [jnp.where(lane_head == h, kmean, 0.0) for h in range(N_HEADS_B)], axis=0)
    q_hi, q_lo = _split_bf16(qb)
    w_hi, w_lo = _split_bf16(wt)
    gate_ref[0] = _dot_nt(w_hi, q_hi) + (_dot_nt(w_lo, q_hi) + _dot_nt(w_hi, q_lo))


def _project(x, g_attn, w_in_bf16):
    b, s, d = x.shape
    n_blocks = s // MOBA_BLOCK
    return pl.pallas_call(
        _proj_kernel,
        grid=(b, s // PROJ_ROWS),
        in_specs=[
            pl.BlockSpec((1, PROJ_ROWS, d), lambda i, t: (i, t, 0)),
            pl.BlockSpec((1, d), lambda i, t: (0, 0)),
            pl.BlockSpec((d, 3 * d), lambda i, t: (0, 0)),
        ],
        out_specs=[
            pl.BlockSpec((1, PROJ_ROWS, 3 * d), lambda i, t: (i, t, 0)),
            pl.BlockSpec((1, LANES, PROJ_ROWS), lambda i, t: (i, 0, t)),
        ],
        out_shape=[
            jax.ShapeDtypeStruct((b, s, 3 * d), jnp.bfloat16),
            jax.ShapeDtypeStruct((b, LANES, s), jnp.float32),
        ],
        scratch_shapes=[pltpu.VMEM((n_blocks, WIDTH_B), jnp.float32)],
        compiler_params=pltpu.CompilerParams(
            dimension_semantics=("arbitrary", "arbitrary"), vmem_limit_bytes=VMEM_LIMIT),
        name="proj",
    )(x, g_attn, w_in_bf16)


def _select_kernel(gate_ref, sel_ref):
    n_blocks = LANES // N_HEADS_B
    width = gate_ref.shape[2]
    q_pos = pl.program_id(1) * width + lax.broadcasted_iota(jnp.int32, (n_blocks, width), 1)
    blk = lax.broadcasted_iota(jnp.int32, (n_blocks, width), 0)
    past = blk < q_pos // MOBA_BLOCK
    for h in range(N_HEADS_B):
        g = jnp.where(past, gate_ref[0, h * n_blocks:(h + 1) * n_blocks, :], NEG_INF)
        rank = jnp.zeros(g.shape, jnp.int32)
        for other_blk in range(n_blocks):
            other = g[other_blk:other_blk + 1, :]
            beats = (other > g) | ((other == g) & (other_blk < blk))
            rank = rank + beats.astype(jnp.int32)
        sel_ref[0, h * n_blocks:(h + 1) * n_blocks, :] = (past & (rank < MOBA_TOPK)).astype(sel_ref.dtype)


def _select(gate):
    b, _, s = gate.shape
    width = 1024
    return pl.pallas_call(
        _select_kernel,
        grid=(b, s // width),
        in_specs=[pl.BlockSpec((1, LANES, width), lambda i, t: (i, 0, t))],
        out_specs=pl.BlockSpec((1, LANES, width), lambda i, t: (i, 0, t)),
        out_shape=jax.ShapeDtypeStruct((b, LANES, s), jnp.float32),
        compiler_params=pltpu.CompilerParams(dimension_semantics=("arbitrary", "arbitrary")),
        name="select",
    )(gate)


def _dilated_kernel(slopes_ref, q1_ref, k1_ref, v1_ref, q4_ref, k4_ref, v4_ref,
                    q16_ref, k16_ref, v16_ref, o_ref, acc_ref, m_ref, l_ref):
    pair = pl.program_id(1)
    seq = acc_ref.shape[0]
    c = DIL_BLOCK
    acc_ref[...] = jnp.zeros_like(acc_ref)
    m_ref[...] = jnp.full_like(m_ref, NEG_INF)
    l_ref[...] = jnp.zeros_like(l_ref)

    lane = lax.broadcasted_iota(jnp.int32, (c, LANES), 1)
    head0 = lane < HEAD_DIM
    rel = (lax.broadcasted_iota(jnp.int32, (c, 2 * c), 0)
           - lax.broadcasted_iota(jnp.int32, (c, 2 * c), 1))

    branches = ((1, q1_ref, k1_ref, v1_ref), (4, q4_ref, k4_ref, v4_ref), (16, q16_ref, k16_ref, v16_ref))
    for dil, q_ref, k_ref, v_ref in branches:
        n_chunks = seq // (dil * c)

        def block_step(idx, carry, dil=dil, q_ref=q_ref, k_ref=k_ref, v_ref=v_ref, n_chunks=n_chunks):
            r = idx // n_chunks
            n = idx % n_chunks
            q0 = pl.multiple_of(n * c, c)
            k0 = pl.multiple_of(jnp.maximum(n - 1, 0) * c, c)
            dist = rel + (q0 - k0)
            valid = (dist >= 0) & (dist <= c)
            distf = dist.astype(jnp.float32)
            q = q_ref[0, r, pl.ds(q0, c), :]
            k = k_ref[0, r, pl.ds(k0, 2 * c), :]
            v = v_ref[0, r, pl.ds(k0, 2 * c), :]
            if dil == 1:
                rows = pl.ds(q0, c)
            else:
                rows = pl.ds(r + dil * q0, c, stride=dil)
            alphas, pvs = [], []
            for h in range(2):
                slope = slopes_ref[2 * pair + h] * dil
                qh = jnp.where(head0 if h == 0 else ~head0, q, jnp.zeros_like(q))
                s = _dot_nt(qh, k)
                s = jnp.where(valid, s - slope * distf, NEG_INF)
                m_old = m_ref[h, rows, :][:, 0:1]
                l_old = l_ref[h, rows, :][:, 0:1]
                m_new = jnp.maximum(m_old, jnp.max(s, axis=-1, keepdims=True))
                p = jnp.exp2(s - m_new)
                alpha = jnp.exp2(m_old - m_new)
                l_new = alpha * l_old + jnp.sum(p, axis=-1, keepdims=True)
                m_ref[h, rows, :] = jnp.broadcast_to(m_new, (c, LANES))
                l_ref[h, rows, :] = jnp.broadcast_to(l_new, (c, LANES))
                alphas.append(alpha)
                pvs.append(jnp.dot(p.astype(v.dtype), v, preferred_element_type=jnp.float32))
            alpha_l = jnp.where(head0, alphas[0], alphas[1])
            pv_l = jnp.where(head0, pvs[0], pvs[1])
            acc_ref[rows, :] = acc_ref[rows, :] * alpha_l + pv_l
            return carry

        lax.fori_loop(0, dil * n_chunks, block_step, 0)

    lane_s = lax.broadcasted_iota(jnp.int32, (seq, LANES), 1)
    l_all = jnp.where(lane_s < HEAD_DIM, l_ref[0], l_ref[1])
    o_ref[0] = (acc_ref[...] / l_all).astype(o_ref.dtype)


def _dilated(qkv, qkv4, qkv16, slopes):
    b, s, _ = qkv.shape
    n_pairs = WIDTH_A // LANES
    col = lambda part: (lambda i, p: (i, 0, part * n_pairs + p))
    colp = lambda part: (lambda i, p: (i, 0, 0, part * n_pairs + p))
    in_specs = [pl.BlockSpec(memory_space=pltpu.SMEM)]
    in_specs += [pl.BlockSpec((1, 1, s, LANES), colp(part)) for part in range(3)]
    in_specs += [pl.BlockSpec((1, 4, s // 4, LANES), colp(part)) for part in range(3)]
    in_specs += [pl.BlockSpec((1, 16, s // 16, LANES), colp(part)) for part in range(3)]
    qkv1 = qkv.reshape(b, 1, s, qkv.shape[-1])
    return pl.pallas_call(
        _dilated_kernel,
        grid=(b, n_pairs),
        in_specs=in_specs,
        out_specs=pl.BlockSpec((1, s, LANES), lambda i, p: (i, 0, p)),
        out_shape=jax.ShapeDtypeStruct((b, s, WIDTH_A), jnp.float32),
        scratch_shapes=[pltpu.VMEM((s, LANES), jnp.float32),
                        pltpu.VMEM((2, s, LANES), jnp.float32),
                        pltpu.VMEM((2, s, LANES), jnp.float32)],
        compiler_params=pltpu.CompilerParams(
            dimension_semantics=("arbitrary", "arbitrary"), vmem_limit_bytes=VMEM_LIMIT),
        name="dilated",
    )(slopes, qkv1, qkv1, qkv1, qkv4, qkv4, qkv4, qkv16, qkv16, qkv16)


MOBA_STAGES = 3
MOBA_STAT_SLOTS = 4
M_INIT = -1e30


def _moba_pair_tables(n_blocks):
    pairs = [(i, j) for i in range(n_blocks) for j in range(i + 1)]
    assert len(pairs) % 2 == 0
    pairs += [pairs[-1]] * (MOBA_STAGES - 1)
    return (np.array([p[0] for p in pairs], np.int32), np.array([p[1] for p in pairs], np.int32),
            np.array([int(p[0] == p[1]) for p in pairs], np.int32))


def _moba_kernel(slopes_ref, qblk_ref, kblk_ref, own_ref, qt_ref, k_ref, vt_ref, sel_ref, o_ref,
                 acc_ref, m_ref, l_ref, bias_ref, s_ref, p_ref, mloc_ref, lloc_ref):
    pair = pl.program_id(1)
    bs = MOBA_BLOCK
    n_blocks = k_ref.shape[1] // bs
    n_steps = qblk_ref.shape[0] - (MOBA_STAGES - 1)
    sel_rows = LANES // N_HEADS_B
    slopes = (slopes_ref[2 * pair], slopes_ref[2 * pair + 1])
    rel = (lax.broadcasted_iota(jnp.int32, (bs, bs), 0)
           - lax.broadcasted_iota(jnp.int32, (bs, bs), 1))
    relf = rel.astype(jnp.float32)
    for h in range(2):
        bias_ref[h] = slopes[h] * relf
        bias_ref[2 + h] = jnp.where(rel <= 0, slopes[h] * relf, NEG_INF)
    m_ref[...] = jnp.full_like(m_ref, M_INIT)
    l_ref[...] = jnp.zeros_like(l_ref)
    acc_ref[...] = jnp.zeros_like(acc_ref)
    no_head = jnp.zeros((HEAD_DIM, bs), qt_ref.dtype)

    def scores(t, slot):
        q0 = pl.multiple_of(qblk_ref[t] * bs, bs)
        k0 = pl.multiple_of(kblk_ref[t] * bs, bs)
        stat = jnp.bitwise_and(t, MOBA_STAT_SLOTS - 1)
        qt = qt_ref[0, :, pl.ds(q0, bs)]
        kj = k_ref[0, pl.ds(k0, bs), :]
        qt_heads = (jnp.concatenate([qt[:HEAD_DIM], no_head], axis=0),
                    jnp.concatenate([no_head, qt[HEAD_DIM:]], axis=0))
        for h in range(2):
            st = (jnp.dot(kj, qt_heads[h], preferred_element_type=jnp.float32)
                  + bias_ref[2 * own_ref[t] + h])
            s_ref[slot, h] = st
            mloc_ref[stat, h:h + 1, :] = jnp.max(st, axis=0, keepdims=True)

    def softmax(t, slot):
        stat = jnp.bitwise_and(t, MOBA_STAT_SLOTS - 1)
        for h in range(2):
            p = jnp.exp2(s_ref[slot, h] - mloc_ref[stat, h:h + 1, :])
            lloc_ref[stat, h:h + 1, :] = jnp.sum(p, axis=0, keepdims=True)
            p_ref[slot, h] = p.astype(p_ref.dtype)

    def merge(t, slot):
        i = qblk_ref[t]
        j = kblk_ref[t]
        q0 = pl.multiple_of(i * bs, bs)
        k0 = pl.multiple_of(j * bs, bs)
        stat = jnp.bitwise_and(t, MOBA_STAT_SLOTS - 1)
        gap = ((i - j) * bs).astype(jnp.float32)
        is_own = own_ref[t] > 0
        for h in range(2):
            vt = vt_ref[0, h * HEAD_DIM:(h + 1) * HEAD_DIM, pl.ds(k0, bs)]
            pv = jnp.dot(vt, p_ref[slot, h], preferred_element_type=jnp.float32)
            chosen = (sel_ref[0, pl.ds(h * sel_rows + j, 1), pl.ds(q0, bs)] > 0.5) | is_own
            m_blk = mloc_ref[stat, h:h + 1, :] - slopes[h] * gap
            m_old = m_ref[h:h + 1, pl.ds(q0, bs)]
            m_new = jnp.where(chosen, jnp.maximum(m_old, m_blk), m_old)
            a_old = jnp.exp2(m_old - m_new)
            a_blk = jnp.where(chosen, jnp.exp2(m_blk - m_new), 0.0)
            acc_ref[h, :, pl.ds(q0, bs)] = a_old * acc_ref[h, :, pl.ds(q0, bs)] + a_blk * pv
            l_ref[h:h + 1, pl.ds(q0, bs)] = (a_old * l_ref[h:h + 1, pl.ds(q0, bs)]
                                             + a_blk * lloc_ref[stat, h:h + 1, :])
            m_ref[h:h + 1, pl.ds(q0, bs)] = m_new

    scores(0, 0)
    scores(1, 1)
    softmax(0, 0)

    def pipeline(it, carry):
        for u in range(2):
            t = 2 * it + u
            scores(t + 2, u)
            softmax(t + 1, 1 - u)
            merge(t, u)
        return carry

    lax.fori_loop(0, n_steps // 2, pipeline, 0)

    def finish(i, carry):
        q0 = pl.multiple_of(i * bs, bs)
        halves = [acc_ref[h, :, pl.ds(q0, bs)] / l_ref[h:h + 1, pl.ds(q0, bs)] for h in range(2)]
        o_ref[0, pl.ds(q0, bs), :] = jnp.concatenate(halves, axis=0).T.astype(o_ref.dtype)
        return carry

    lax.fori_loop(0, n_blocks, finish, 0)


def _moba(qt, qkv, vt, sel, slopes):
    b, s, _ = qkv.shape
    n_pairs = WIDTH_B // LANES
    k_base = (3 * WIDTH_A + WIDTH_B) // LANES
    sel_rows = 2 * (LANES // N_HEADS_B)
    qblk, kblk, own = _moba_pair_tables(s // MOBA_BLOCK)
    smem = pl.BlockSpec(memory_space=pltpu.SMEM)
    return pl.pallas_call(
        _moba_kernel,
        grid=(b, n_pairs),
        in_specs=[
            smem, smem, smem, smem,
            pl.BlockSpec((1, LANES, s), lambda i, p: (i, p, 0)),
            pl.BlockSpec((1, s, LANES), lambda i, p: (i, 0, k_base + p)),
            pl.BlockSpec((1, LANES, s), lambda i, p: (i, p, 0)),
            pl.BlockSpec((1, sel_rows, s), lambda i, p: (i, p, 0)),
        ],
        out_specs=pl.BlockSpec((1, s, LANES), lambda i, p: (i, 0, p)),
        out_shape=jax.ShapeDtypeStruct((b, s, WIDTH_B), jnp.float32),
        scratch_shapes=[pltpu.VMEM((2, HEAD_DIM, s), jnp.float32),
                        pltpu.VMEM((8, s), jnp.float32),
                        pltpu.VMEM((8, s), jnp.float32),
                        pltpu.VMEM((4, MOBA_BLOCK, MOBA_BLOCK), jnp.float32),
                        pltpu.VMEM((2, 2, MOBA_BLOCK, MOBA_BLOCK), jnp.float32),
                        pltpu.VMEM((2, 2, MOBA_BLOCK, MOBA_BLOCK), jnp.bfloat16),
                        pltpu.VMEM((MOBA_STAT_SLOTS, 8, MOBA_BLOCK), jnp.float32),
                        pltpu.VMEM((MOBA_STAT_SLOTS, 8, MOBA_BLOCK), jnp.float32)],
        compiler_params=pltpu.CompilerParams(
            dimension_semantics=("arbitrary", "arbitrary"), vmem_limit_bytes=VMEM_LIMIT),
        name="moba",
    )(slopes, jnp.asarray(qblk), jnp.asarray(kblk), jnp.asarray(own), qt, qkv, vt, sel)


def _rms(x, g):
    return (x * lax.rsqrt(jnp.mean(x * x, axis=-1, keepdims=True) + EPS)) * g


def _tail_kernel(x_ref, p_ref, oa_ref, ob_ref, ga_ref, gb_ref, wout_ref, gmlp_ref, wup_ref, wdown_ref,
                 gple_ref, wgate_ref, bgate_ref, wproj_ref, gfin_ref, o_ref):
    bf = jnp.bfloat16
    ya = _rms(oa_ref[0].astype(jnp.float32), ga_ref[...]).astype(bf)
    yb = _rms(ob_ref[0].astype(jnp.float32), gb_ref[...]).astype(bf)
    y = (jnp.dot(ya, wout_ref[0:WIDTH_A, :], preferred_element_type=jnp.float32)
         + jnp.dot(yb, wout_ref[WIDTH_A:, :], preferred_element_type=jnp.float32))
    h = x_ref[0] + y

    hn = _rms(h, gmlp_ref[...]).astype(bf)
    ff_chunk = 1024
    for c in range(D_FF // ff_chunk):
        u = jnp.dot(hn, wup_ref[:, c * ff_chunk:(c + 1) * ff_chunk], preferred_element_type=jnp.float32)
        u = jnp.square(jnp.maximum(u, 0.0)).astype(bf)
        h = h + jnp.dot(u, wdown_ref[c * ff_chunk:(c + 1) * ff_chunk, :], preferred_element_type=jnp.float32)

    z = jnp.dot(_rms(h, gple_ref[...]).astype(bf), wgate_ref[...], preferred_element_type=jnp.float32)
    gate = jax.nn.sigmoid(z + bgate_ref[...])
    h = h + gate * jnp.dot(p_ref[0].astype(bf), wproj_ref[...], preferred_element_type=jnp.float32)
    o_ref[0] = _rms(h, gfin_ref[...]).astype(o_ref.dtype)


def _tail(x, p, oa, ob, ga, gb, wout, gmlp, wup, wdown, gple, wgate, bgate, wproj, gfin):
    b, s, d = x.shape
    rows = lambda width: pl.BlockSpec((1, TAIL_ROWS, width), lambda i, t: (i, t, 0))
    full = lambda arr: pl.BlockSpec(arr.shape, lambda i, t: (0,) * arr.ndim,
                                    pipeline_mode=pl.Buffered(1))
    consts = (ga, gb, wout, gmlp, wup, wdown, gple, wgate, bgate, wproj, gfin)
    return pl.pallas_call(
        _tail_kernel,
        grid=(b, s // TAIL_ROWS),
        in_specs=[rows(d), rows(PLE_DIM), rows(WIDTH_A), rows(WIDTH_B)] + [full(a) for a in consts],
        out_specs=rows(d),
        out_shape=jax.ShapeDtypeStruct((b, s, d), x.dtype),
        compiler_params=pltpu.CompilerParams(
            dimension_semantics=("arbitrary", "arbitrary"), vmem_limit_bytes=VMEM_LIMIT),
        name="tail",
    )(x, p, oa, ob, *consts)


def kernel(x, p, g_attn, w_in, g_out_a, g_out_b, w_out, g_mlp, w_up, w_down, g_ple, w_ple_gate,
           b_ple_gate, w_ple_proj, g_final):
    assert x.shape[1:] == (4096, D_MODEL) and p.shape[0] == 1 and w_in.shape[0] == 1
    b, s, _ = x.shape
    bf = jnp.bfloat16
    slopes_a, slopes_b = _alibi_log2_slopes()
    row = lambda v: v.reshape(1, -1)

    qkv, gate = _project(x, row(g_attn[0]), w_in[0].astype(bf))
    sel = _select(gate)

    qkv_a = qkv[..., :3 * WIDTH_A]
    perm = lambda dil: qkv_a.reshape(b, s // dil, dil, 3 * WIDTH_A).transpose(0, 2, 1, 3)
    oa = _dilated(qkv, perm(4), perm(16), jnp.asarray(slopes_a))
    b_base = 3 * WIDTH_A
    qbt = qkv[..., b_base:b_base + WIDTH_B].transpose(0, 2, 1)
    vbt = qkv[..., b_base + 2 * WIDTH_B:].transpose(0, 2, 1)
    ob = _moba(qbt, qkv, vbt, sel, jnp.asarray(slopes_b))

    return _tail(x, p[0], oa, ob, row(g_out_a[0]), row(g_out_b[0]), w_out[0].astype(bf), row(g_mlp[0]),
                 w_up[0].astype(bf), w_down[0].astype(bf), row(g_ple[0]), w_ple_gate[0].astype(bf),
                 row(b_ple_gate[0]), w_ple_proj[0].astype(bf), row(g_final))
```

```python
import functools
import math

import jax
import jax.numpy as jnp
import numpy as np
from jax import lax
from jax.experimental import pallas as pl
from jax.experimental.pallas import tpu as pltpu

D_MODEL = 1024
HEAD_DIM = 64
N_HEADS = D_MODEL // HEAD_DIM
N_HEADS_A = N_HEADS // 2
N_HEADS_B = N_HEADS - N_HEADS_A
WIDTH_A = N_HEADS_A * HEAD_DIM
WIDTH_B = N_HEADS_B * HEAD_DIM
DILATED_PAIRS = ((128, 1), (512, 4), (2048, 16))
MOBA_BLOCK = 256
MOBA_TOPK = 3
D_FF = 4 * D_MODEL
PLE_DIM = 256
EPS = 1e-6

LANES = 128
LOG2E = 1.4426950408889634
Q_SCALE = HEAD_DIM ** -0.5 * LOG2E
NEG_INF = float("-inf")
VMEM_LIMIT = 56 * 1024 * 1024

PROJ_ROWS = 512
TAIL_ROWS = 512
DIL_BLOCK = 128


def _alibi_log2_slopes():
    s = np.array([2.0 ** (-8.0 * (i + 1) / N_HEADS) for i in range(N_HEADS)], dtype=np.float64)
    return (s[0::2] * LOG2E).astype(np.float32), (s[1::2] * LOG2E).astype(np.float32)


def _dot_nt(a, b):
    return lax.dot_general(a, b, (((1,), (1,)), ((), ())), preferred_element_type=jnp.float32)


def _split_bf16(x):
    hi = x.astype(jnp.bfloat16)
    lo = (x - hi.astype(jnp.float32)).astype(jnp.bfloat16)
    return hi, lo


PIPE_LAG = 2
PIPE_STEPS = 2 * PIPE_LAG


def _software_pipeline(n_steps, scores, softmax, emit):
    assert n_steps % PIPE_STEPS == 0
    for t in range(2 * PIPE_LAG):
        scores(t, t % PIPE_STEPS)
    for t in range(PIPE_LAG):
        softmax(t, t % PIPE_STEPS)

    def body(it, carry):
        for k in range(PIPE_STEPS):
            t = PIPE_STEPS * it + k
            scores(t + 2 * PIPE_LAG, k)
            softmax(t + PIPE_LAG, (k + PIPE_LAG) % PIPE_STEPS)
            emit(t, k)
        return carry

    lax.fori_loop(0, n_steps // PIPE_STEPS, body, 0)


def _proj_kernel(x_ref, g_ref, w_ref, qkv_ref, gate_ref, kmean_ref):
    t = pl.program_id(1)
    x = x_ref[0]
    ms = jnp.mean(x * x, axis=-1, keepdims=True)
    hn = (x * lax.rsqrt(ms + EPS)) * g_ref[...]
    qkv = jnp.dot(hn.astype(jnp.bfloat16), w_ref[...], preferred_element_type=jnp.float32)

    qa = qkv[:, 0:WIDTH_A]
    qb = qkv[:, 3 * WIDTH_A:3 * WIDTH_A + WIDTH_B]
    kb = qkv[:, 3 * WIDTH_A + WIDTH_B:3 * WIDTH_A + 2 * WIDTH_B]
    qkv_ref[0, :, 0:WIDTH_A] = (qa * Q_SCALE).astype(qkv_ref.dtype)
    qkv_ref[0, :, WIDTH_A:3 * WIDTH_A] = qkv[:, WIDTH_A:3 * WIDTH_A].astype(qkv_ref.dtype)
    qkv_ref[0, :, 3 * WIDTH_A:3 * WIDTH_A + WIDTH_B] = (qb * Q_SCALE).astype(qkv_ref.dtype)
    qkv_ref[0, :, 3 * WIDTH_A + WIDTH_B:] = qkv[:, 3 * WIDTH_A + WIDTH_B:].astype(qkv_ref.dtype)

    @pl.when(t == 0)
    def _():
        kmean_ref[...] = jnp.zeros_like(kmean_ref)

    blocks_per_tile = PROJ_ROWS // MOBA_BLOCK
    for sb in range(blocks_per_tile):
        km = jnp.mean(kb[sb * MOBA_BLOCK:(sb + 1) * MOBA_BLOCK, :], axis=0, keepdims=True)
        kmean_ref[pl.ds(t * blocks_per_tile + sb, 1), :] = km

    kmean = kmean_ref[...]
    lane_head = lax.broadcasted_iota(jnp.int32, kmean.shape, 1) // HEAD_DIM
    wt = jnp.concatenate(
        [jnp.where(lane_head == h, kmean, 0.0) for h in range(N_HEADS_B)], axis=0)
    q_hi, q_lo = _split_bf16(qb)
    w_hi, w_lo = _split_bf16(wt)
    gate_ref[0] = _dot_nt(w_hi, q_hi) + (_dot_nt(w_lo, q_hi) + _dot_nt(w_hi, q_lo))


def _project(x, g_attn, w_in_bf16):
    b, s, d = x.shape
    n_blocks = s // MOBA_BLOCK
    return pl.pallas_call(
        _proj_kernel,
        grid=(b, s // PROJ_ROWS),
        in_specs=[
            pl.BlockSpec((1, PROJ_ROWS, d), lambda i, t: (i, t, 0)),
            pl.BlockSpec((1, d), lambda i, t: (0, 0)),
            pl.BlockSpec((d, 3 * d), lambda i, t: (0, 0)),
        ],
        out_specs=[
            pl.BlockSpec((1, PROJ_ROWS, 3 * d), lambda i, t: (i, t, 0)),
            pl.BlockSpec((1, LANES, PROJ_ROWS), lambda i, t: (i, 0, t)),
        ],
        out_shape=[
            jax.ShapeDtypeStruct((b, s, 3 * d), jnp.bfloat16),
            jax.ShapeDtypeStruct((b, LANES, s), jnp.float32),
        ],
        scratch_shapes=[pltpu.VMEM((n_blocks, WIDTH_B), jnp.float32)],
        compiler_params=pltpu.CompilerParams(
            dimension_semantics=("arbitrary", "arbitrary"), vmem_limit_bytes=VMEM_LIMIT),
        name="proj",
    )(x, g_attn, w_in_bf16)


def _select_kernel(gate_ref, sel_ref):
    n_blocks = LANES // N_HEADS_B
    width = gate_ref.shape[2]
    q_pos = pl.program_id(1) * width + lax.broadcasted_iota(jnp.int32, (n_blocks, width), 1)
    blk = lax.broadcasted_iota(jnp.int32, (n_blocks, width), 0)
    past = blk < q_pos // MOBA_BLOCK
    for h in range(N_HEADS_B):
        g = jnp.where(past, gate_ref[0, h * n_blocks:(h + 1) * n_blocks, :], NEG_INF)
        rank = jnp.zeros(g.shape, jnp.int32)
        for other_blk in range(n_blocks):
            other = g[other_blk:other_blk + 1, :]
            beats = (other > g) | ((other == g) & (other_blk < blk))
            rank = rank + beats.astype(jnp.int32)
        sel_ref[0, h * n_blocks:(h + 1) * n_blocks, :] = (past & (rank < MOBA_TOPK)).astype(sel_ref.dtype)


def _select(gate):
    b, _, s = gate.shape
    width = 1024
    return pl.pallas_call(
        _select_kernel,
        grid=(b, s // width),
        in_specs=[pl.BlockSpec((1, LANES, width), lambda i, t: (i, 0, t))],
        out_specs=pl.BlockSpec((1, LANES, width), lambda i, t: (i, 0, t)),
        out_shape=jax.ShapeDtypeStruct((b, LANES, s), jnp.float32),
        compiler_params=pltpu.CompilerParams(dimension_semantics=("arbitrary", "arbitrary")),
        name="select",
    )(gate)


def _dilated_kernel(slopes_ref, qt1_ref, k1_ref, vt1_ref, qt4_ref, k4_ref, vt4_ref,
                    qt16_ref, k16_ref, vt16_ref, o_ref,
                    acc_ref, stat_ref, bias_ref, s_ref, p_ref, mloc_ref, stats_ref):
    pair = pl.program_id(0)
    seq = k1_ref.shape[1]
    c = DIL_BLOCK
    n_steps = seq // c
    dilations = tuple(d for _, d in DILATED_PAIRS)

    @pl.when(pl.program_id(1) == 0)
    def _():
        key = lax.broadcasted_iota(jnp.int32, (2 * c, 2 * c), 0)
        lane = lax.broadcasted_iota(jnp.int32, (2 * c, 2 * c), 1)
        slope_l = jnp.where(lane >= c, slopes_ref[2 * pair + 1], slopes_ref[2 * pair])
        for b_idx, dil in enumerate(dilations):
            for first in range(2):
                dist = (0 if first else c) + jnp.bitwise_and(lane, c - 1) - key
                bias_ref[2 * b_idx + first] = jnp.where(
                    (dist >= 0) & (dist <= c), (-dil) * slope_l * dist.astype(jnp.float32), NEG_INF)

    no_head = jnp.zeros((HEAD_DIM, c), qt1_ref.dtype)

    branches = ((qt1_ref, k1_ref, vt1_ref), (qt4_ref, k4_ref, vt4_ref), (qt16_ref, k16_ref, vt16_ref))
    for b_idx, (dil, (qt_ref, k_ref, vt_ref)) in enumerate(zip(dilations, branches)):
        sub_len = seq // dil
        n_chunks = sub_len // c

        def offsets(t, n_chunks=n_chunks, sub_len=sub_len):
            t = jnp.minimum(t, n_steps - 1)
            r = t // n_chunks
            n = t % n_chunks
            q_off = pl.multiple_of(r * sub_len + n * c, c)
            k_off = pl.multiple_of(r * sub_len + jnp.maximum(n - 1, 0) * c, c)
            return r, n, q_off, k_off

        def scores(t, slot, b_idx=b_idx, qt_ref=qt_ref, k_ref=k_ref, offsets=offsets):
            _, n, q_off, k_off = offsets(t)
            qt = qt_ref[0, :, pl.ds(q_off, c)]
            rhs = jnp.concatenate([jnp.concatenate([qt[:HEAD_DIM], no_head], axis=0),
                                   jnp.concatenate([no_head, qt[HEAD_DIM:]], axis=0)], axis=1)
            keys = k_ref[0, pl.ds(k_off, 2 * c), :]
            first = (n == 0).astype(jnp.int32)
            st = jnp.dot(keys, rhs, preferred_element_type=jnp.float32) + bias_ref[2 * b_idx + first]
            s_ref[slot] = st
            mloc_ref[slot, 0:1, :] = jnp.max(st, axis=0, keepdims=True)

        def softmax(t, slot):
            m_loc = mloc_ref[slot, 0:1, :]
            p = jnp.exp2(s_ref[slot] - m_loc)
            stats_ref[slot, 0:1, :] = m_loc
            stats_ref[slot, 1:2, :] = jnp.sum(p, axis=0, keepdims=True)
            p_ref[slot] = p.astype(p_ref.dtype)

        def emit(t, slot, b_idx=b_idx, dil=dil, vt_ref=vt_ref, offsets=offsets):
            r, n, _, k_off = offsets(t)
            vt = vt_ref[0, :, pl.ds(k_off, 2 * c)]
            pv = jnp.dot(vt, p_ref[slot], preferred_element_type=jnp.float32)
            out_t = jnp.concatenate([pv[:HEAD_DIM, :c], pv[HEAD_DIM:, c:]], axis=0)
            if dil == 1:
                rows = pl.ds(pl.multiple_of(n * c, c), c)
            else:
                rows = pl.ds(r + dil * c * n, c, stride=dil)
            l_loc = stats_ref[slot, 1:2, :]
            inv_l = 1.0 / l_loc
            lse = stats_ref[slot, 0:1, :] + jnp.log2(l_loc)
            out_t = out_t * jnp.concatenate([jnp.broadcast_to(inv_l[:, :c], (HEAD_DIM, c)),
                                             jnp.broadcast_to(inv_l[:, c:], (HEAD_DIM, c))], axis=0)
            lse_t = jnp.concatenate([jnp.broadcast_to(lse[:, :c], (HEAD_DIM, c)),
                                     jnp.broadcast_to(lse[:, c:], (HEAD_DIM, c))], axis=0)
            acc_ref[b_idx, rows, :] = out_t.T
            stat_ref[b_idx, rows, :] = lse_t.T

        _software_pipeline(n_steps, scores, softmax, emit)

    def merge(ci, carry):
        rows = pl.ds(pl.multiple_of(ci * c, c), c)
        lses = [stat_ref[b_idx, rows, :] for b_idx in range(len(dilations))]
        lse_max = functools.reduce(jnp.maximum, lses)
        weights = [jnp.exp2(lse - lse_max) for lse in lses]
        out = functools.reduce(lambda a, b: a + b,
                               [w * acc_ref[b_idx, rows, :] for b_idx, w in enumerate(weights)])
        o_ref[0, rows, :] = (out / functools.reduce(lambda a, b: a + b, weights)).astype(o_ref.dtype)
        return carry

    lax.fori_loop(0, seq // c, merge, 0)


def _dilated(q_t, k_p, v_t, slopes):
    b, s, _ = k_p[0].shape
    n_pairs = WIDTH_A // LANES
    feat_major = pl.BlockSpec((1, LANES, s), lambda p, i: (i, p, 0))
    tok_major = pl.BlockSpec((1, s, LANES), lambda p, i: (i, 0, p))
    operands, in_specs = [slopes], [pl.BlockSpec(memory_space=pltpu.SMEM)]
    for q, k, v in zip(q_t, k_p, v_t):
        operands += [q, k, v]
        in_specs += [feat_major, tok_major, feat_major]
    n_br = len(DILATED_PAIRS)
    c = DIL_BLOCK
    return pl.pallas_call(
        _dilated_kernel,
        grid=(n_pairs, b),
        in_specs=in_specs,
        out_specs=tok_major,
        out_shape=jax.ShapeDtypeStruct((b, s, WIDTH_A), jnp.float32),
        scratch_shapes=[pltpu.VMEM((n_br, s, LANES), jnp.float32),
                        pltpu.VMEM((n_br, s, LANES), jnp.float32),
                        pltpu.VMEM((2 * n_br, 2 * c, 2 * c), jnp.float32),
                        pltpu.VMEM((PIPE_STEPS, 2 * c, 2 * c), jnp.float32),
                        pltpu.VMEM((PIPE_STEPS, 2 * c, 2 * c), jnp.bfloat16),
                        pltpu.VMEM((PIPE_STEPS, 8, 2 * c), jnp.float32),
                        pltpu.VMEM((PIPE_STEPS, 8, 2 * c), jnp.float32)],
        compiler_params=pltpu.CompilerParams(
            dimension_semantics=("arbitrary", "arbitrary"), vmem_limit_bytes=VMEM_LIMIT),
        name="dilated",
    )(*operands)


MOBA_PAD_STEPS = 2 * PIPE_LAG
M_INIT = -1e30


def _moba_pair_tables(n_blocks):
    pairs = [(i, j) for i in range(n_blocks) for j in range(i + 1)]
    pairs += [pairs[-1]] * MOBA_PAD_STEPS
    return (np.array([p[0] for p in pairs], np.int32), np.array([p[1] for p in pairs], np.int32),
            np.array([int(p[0] == p[1]) for p in pairs], np.int32))


def _moba_kernel(slopes_ref, qblk_ref, kblk_ref, own_ref, qt_ref, k_ref, vt_ref, sel_ref, o_ref,
                 acc_ref, m_ref, l_ref, bias_ref, s_ref, p_ref, mloc_ref, stats_ref):
    pair = pl.program_id(0)
    bs = MOBA_BLOCK
    n_blocks = k_ref.shape[1] // bs
    n_steps = qblk_ref.shape[0] - MOBA_PAD_STEPS
    sel_rows = LANES // N_HEADS_B
    slopes = (slopes_ref[2 * pair], slopes_ref[2 * pair + 1])

    @pl.when(pl.program_id(1) == 0)
    def _():
        rel = (lax.broadcasted_iota(jnp.int32, (bs, bs), 0)
               - lax.broadcasted_iota(jnp.int32, (bs, bs), 1))
        relf = rel.astype(jnp.float32)
        for h in range(2):
            bias_ref[h] = slopes[h] * relf
            bias_ref[2 + h] = jnp.where(rel <= 0, slopes[h] * relf, NEG_INF)

    m_ref[...] = jnp.full_like(m_ref, M_INIT)
    l_ref[...] = jnp.zeros_like(l_ref)
    acc_ref[...] = jnp.zeros_like(acc_ref)
    no_head = jnp.zeros((HEAD_DIM, bs), qt_ref.dtype)

    def scores(t, slot):
        q0 = pl.multiple_of(qblk_ref[t] * bs, bs)
        k0 = pl.multiple_of(kblk_ref[t] * bs, bs)
        qt = qt_ref[0, :, pl.ds(q0, bs)]
        kj = k_ref[0, pl.ds(k0, bs), :]
        qt_heads = (jnp.concatenate([qt[:HEAD_DIM], no_head], axis=0),
                    jnp.concatenate([no_head, qt[HEAD_DIM:]], axis=0))
        for h in range(2):
            st = (jnp.dot(kj, qt_heads[h], preferred_element_type=jnp.float32)
                  + bias_ref[2 * own_ref[t] + h])
            s_ref[slot, h] = st
            mloc_ref[slot, h:h + 1, :] = jnp.max(st, axis=0, keepdims=True)

    def softmax(t, slot):
        for h in range(2):
            m_loc = mloc_ref[slot, h:h + 1, :]
            p = jnp.exp2(s_ref[slot, h] - m_loc)
            stats_ref[slot, h:h + 1, :] = m_loc
            stats_ref[slot, 2 + h:3 + h, :] = jnp.sum(p, axis=0, keepdims=True)
            p_ref[slot, h] = p.astype(p_ref.dtype)

    def merge(t, slot):
        i = qblk_ref[t]
        j = kblk_ref[t]
        q0 = pl.multiple_of(i * bs, bs)
        k0 = pl.multiple_of(j * bs, bs)
        gap = ((i - j) * bs).astype(jnp.float32)
        is_own = own_ref[t] > 0
        for h in range(2):
            vt = vt_ref[0, h * HEAD_DIM:(h + 1) * HEAD_DIM, pl.ds(k0, bs)]
            pv = jnp.dot(vt, p_ref[slot, h], preferred_element_type=jnp.float32)
            chosen = (sel_ref[0, pl.ds(h * sel_rows + j, 1), pl.ds(q0, bs)] > 0.5) | is_own
            m_blk = stats_ref[slot, h:h + 1, :] - slopes[h] * gap
            m_old = m_ref[h:h + 1, pl.ds(q0, bs)]
            m_new = jnp.where(chosen, jnp.maximum(m_old, m_blk), m_old)
            a_old = jnp.exp2(m_old - m_new)
            a_blk = jnp.where(chosen, jnp.exp2(m_blk - m_new), 0.0)
            acc_ref[h, :, pl.ds(q0, bs)] = a_old * acc_ref[h, :, pl.ds(q0, bs)] + a_blk * pv
            l_ref[h:h + 1, pl.ds(q0, bs)] = (a_old * l_ref[h:h + 1, pl.ds(q0, bs)]
                                             + a_blk * stats_ref[slot, 2 + h:3 + h, :])
            m_ref[h:h + 1, pl.ds(q0, bs)] = m_new

    _software_pipeline(n_steps, scores, softmax, merge)

    def finish(i, carry):
        q0 = pl.multiple_of(i * bs, bs)
        halves = [acc_ref[h, :, pl.ds(q0, bs)] / l_ref[h:h + 1, pl.ds(q0, bs)] for h in range(2)]
        o_ref[0, pl.ds(q0, bs), :] = jnp.concatenate(halves, axis=0).T.astype(o_ref.dtype)
        return carry

    lax.fori_loop(0, n_blocks, finish, 0)


def _moba(qt, qkv, vt, sel, slopes):
    b, s, _ = qkv.shape
    n_pairs = WIDTH_B // LANES
    k_base = (3 * WIDTH_A + WIDTH_B) // LANES
    sel_rows = 2 * (LANES // N_HEADS_B)
    qblk, kblk, own = _moba_pair_tables(s // MOBA_BLOCK)
    smem = pl.BlockSpec(memory_space=pltpu.SMEM)
    return pl.pallas_call(
        _moba_kernel,
        grid=(n_pairs, b),
        in_specs=[
            smem, smem, smem, smem,
            pl.BlockSpec((1, LANES, s), lambda p, i: (i, p, 0)),
            pl.BlockSpec((1, s, LANES), lambda p, i: (i, 0, k_base + p)),
            pl.BlockSpec((1, LANES, s), lambda p, i: (i, p, 0)),
            pl.BlockSpec((1, sel_rows, s), lambda p, i: (i, p, 0)),
        ],
        out_specs=pl.BlockSpec((1, s, LANES), lambda p, i: (i, 0, p)),
        out_shape=jax.ShapeDtypeStruct((b, s, WIDTH_B), jnp.float32),
        scratch_shapes=[pltpu.VMEM((2, HEAD_DIM, s), jnp.float32),
                        pltpu.VMEM((8, s), jnp.float32),
                        pltpu.VMEM((8, s), jnp.float32),
                        pltpu.VMEM((4, MOBA_BLOCK, MOBA_BLOCK), jnp.float32),
                        pltpu.VMEM((PIPE_STEPS, 2, MOBA_BLOCK, MOBA_BLOCK), jnp.float32),
                        pltpu.VMEM((PIPE_STEPS, 2, MOBA_BLOCK, MOBA_BLOCK), jnp.bfloat16),
                        pltpu.VMEM((PIPE_STEPS, 8, MOBA_BLOCK), jnp.float32),
                        pltpu.VMEM((PIPE_STEPS, 8, MOBA_BLOCK), jnp.float32)],
        compiler_params=pltpu.CompilerParams(
            dimension_semantics=("arbitrary", "arbitrary"), vmem_limit_bytes=VMEM_LIMIT),
        name="moba",
    )(slopes, jnp.asarray(qblk), jnp.asarray(kblk), jnp.asarray(own), qt, qkv, vt, sel)


def _rms(x, g):
    return (x * lax.rsqrt(jnp.mean(x * x, axis=-1, keepdims=True) + EPS)) * g


def _tail_kernel(x_ref, p_ref, oa_ref, ob_ref, ga_ref, gb_ref, wout_ref, gmlp_ref, wup_ref, wdown_ref,
                 gple_ref, wgate_ref, bgate_ref, wproj_ref, gfin_ref, o_ref):
    bf = jnp.bfloat16
    ya = _rms(oa_ref[0].astype(jnp.float32), ga_ref[...]).astype(bf)
    yb = _rms(ob_ref[0].astype(jnp.float32), gb_ref[...]).astype(bf)
    y = (jnp.dot(ya, wout_ref[0:WIDTH_A, :], preferred_element_type=jnp.float32)
         + jnp.dot(yb, wout_ref[WIDTH_A:, :], preferred_element_type=jnp.float32))
    h = x_ref[0] + y

    hn = _rms(h, gmlp_ref[...]).astype(bf)
    ff_chunk = 1024
    for c in range(D_FF // ff_chunk):
        u = jnp.dot(hn, wup_ref[:, c * ff_chunk:(c + 1) * ff_chunk], preferred_element_type=jnp.float32)
        u = jnp.square(jnp.maximum(u, 0.0)).astype(bf)
        h = h + jnp.dot(u, wdown_ref[c * ff_chunk:(c + 1) * ff_chunk, :], preferred_element_type=jnp.float32)

    z = jnp.dot(_rms(h, gple_ref[...]).astype(bf), wgate_ref[...], preferred_element_type=jnp.float32)
    gate = jax.nn.sigmoid(z + bgate_ref[...])
    h = h + gate * jnp.dot(p_ref[0].astype(bf), wproj_ref[...], preferred_element_type=jnp.float32)
    o_ref[0] = _rms(h, gfin_ref[...]).astype(o_ref.dtype)


def _tail(x, p, oa, ob, ga, gb, wout, gmlp, wup, wdown, gple, wgate, bgate, wproj, gfin):
    b, s, d = x.shape
    rows = lambda width: pl.BlockSpec((1, TAIL_ROWS, width), lambda i, t: (i, t, 0))
    full = lambda arr: pl.BlockSpec(arr.shape, lambda i, t: (0,) * arr.ndim,
                                    pipeline_mode=pl.Buffered(1))
    consts = (ga, gb, wout, gmlp, wup, wdown, gple, wgate, bgate, wproj, gfin)
    return pl.pallas_call(
        _tail_kernel,
        grid=(b, s // TAIL_ROWS),
        in_specs=[rows(d), rows(PLE_DIM), rows(WIDTH_A), rows(WIDTH_B)] + [full(a) for a in consts],
        out_specs=rows(d),
        out_shape=jax.ShapeDtypeStruct((b, s, d), x.dtype),
        compiler_params=pltpu.CompilerParams(
            dimension_semantics=("arbitrary", "arbitrary"), vmem_limit_bytes=VMEM_LIMIT),
        name="tail",
    )(x, p, oa, ob, *consts)


def kernel(x, p, g_attn, w_in, g_out_a, g_out_b, w_out, g_mlp, w_up, w_down, g_ple, w_ple_gate,
           b_ple_gate, w_ple_proj, g_final):
    assert x.shape[1:] == (4096, D_MODEL) and p.shape[0] == 1 and w_in.shape[0] == 1
    b, s, _ = x.shape
    bf = jnp.bfloat16
    slopes_a, slopes_b = _alibi_log2_slopes()
    row = lambda v: v.reshape(1, -1)

    qkv, gate = _project(x, row(g_attn[0]), w_in[0].astype(bf))
    sel = _select(gate)

    def residue_major(t, dil):
        return t.reshape(b, s // dil, dil, WIDTH_A).transpose(0, 2, 1, 3).reshape(b, s, WIDTH_A)

    dils = [d for _, d in DILATED_PAIRS]
    qa, ka, va = (qkv[..., n * WIDTH_A:(n + 1) * WIDTH_A] for n in range(3))
    oa = _dilated([residue_major(qa, d).transpose(0, 2, 1) for d in dils],
                  [residue_major(ka, d) for d in dils],
                  [residue_major(va, d).transpose(0, 2, 1) for d in dils],
                  jnp.asarray(slopes_a))
    b_base = 3 * WIDTH_A
    qbt = qkv[..., b_base:b_base + WIDTH_B].transpose(0, 2, 1)
    vbt = qkv[..., b_base + 2 * WIDTH_B:].transpose(0, 2, 1)
    ob = _moba(qbt, qkv, vbt, sel, jnp.asarray(slopes_b))

    return _tail(x, p[0], oa, ob, row(g_out_a[0]), row(g_out_b[0]), w_out[0].astype(bf), row(g_mlp[0]),
                 w_up[0].astype(bf), w_down[0].astype(bf), row(g_ple[0]), w_ple_gate[0].astype(bf),
                 row(b_ple_gate[0]), w_ple_proj[0].astype(bf), row(g_final))
```

```python
import functools
import math

import jax
import jax.numpy as jnp
import numpy as np
from jax import lax
from jax.experimental import pallas as pl
from jax.experimental.pallas import tpu as pltpu

D_MODEL = 1024
HEAD_DIM = 64
N_HEADS = D_MODEL // HEAD_DIM
N_HEADS_A = N_HEADS // 2
N_HEADS_B = N_HEADS - N_HEADS_A
WIDTH_A = N_HEADS_A * HEAD_DIM
WIDTH_B = N_HEADS_B * HEAD_DIM
DILATED_PAIRS = ((128, 1), (512, 4), (2048, 16))
MOBA_BLOCK = 256
MOBA_TOPK = 3
D_FF = 4 * D_MODEL
PLE_DIM = 256
EPS = 1e-6

LANES = 128
LOG2E = 1.4426950408889634
Q_SCALE = HEAD_DIM ** -0.5 * LOG2E
NEG_INF = float("-inf")
VMEM_LIMIT = 56 * 1024 * 1024

PROJ_ROWS = 512
TAIL_ROWS = 512
DIL_BLOCK = 128


def _alibi_log2_slopes():
    s = np.array([2.0 ** (-8.0 * (i + 1) / N_HEADS) for i in range(N_HEADS)], dtype=np.float64)
    return (s[0::2] * LOG2E).astype(np.float32), (s[1::2] * LOG2E).astype(np.float32)


def _dot_nt(a, b):
    return lax.dot_general(a, b, (((1,), (1,)), ((), ())), preferred_element_type=jnp.float32)


def _split_bf16(x):
    hi = x.astype(jnp.bfloat16)
    lo = (x - hi.astype(jnp.float32)).astype(jnp.bfloat16)
    return hi, lo


PIPE_LAG = 2
PIPE_STEPS = 2 * PIPE_LAG


def _software_pipeline(n_steps, scores, softmax, emit):
    assert n_steps % PIPE_STEPS == 0
    for t in range(2 * PIPE_LAG):
        scores(t, t % PIPE_STEPS)
    for t in range(PIPE_LAG):
        softmax(t, t % PIPE_STEPS)

    def body(it, carry):
        for k in range(PIPE_STEPS):
            t = PIPE_STEPS * it + k
            scores(t + 2 * PIPE_LAG, k)
            softmax(t + PIPE_LAG, (k + PIPE_LAG) % PIPE_STEPS)
            emit(t, k)
        return carry

    lax.fori_loop(0, n_steps // PIPE_STEPS, body, 0)


def _proj_kernel(x_ref, g_ref, w_ref, qkv_ref, gate_ref, a4_ref, a16_ref, kmean_ref, slab_ref):
    t = pl.program_id(1)
    x = x_ref[0]
    ms = jnp.mean(x * x, axis=-1, keepdims=True)
    hn = (x * lax.rsqrt(ms + EPS)) * g_ref[...]
    qkv = jnp.dot(hn.astype(jnp.bfloat16), w_ref[...], preferred_element_type=jnp.float32)

    qa = qkv[:, 0:WIDTH_A]
    qb = qkv[:, 3 * WIDTH_A:3 * WIDTH_A + WIDTH_B]
    kb = qkv[:, 3 * WIDTH_A + WIDTH_B:3 * WIDTH_A + 2 * WIDTH_B]
    qkv_ref[0, :, 0:WIDTH_A] = (qa * Q_SCALE).astype(qkv_ref.dtype)
    qkv_ref[0, :, WIDTH_A:3 * WIDTH_A] = qkv[:, WIDTH_A:3 * WIDTH_A].astype(qkv_ref.dtype)
    qkv_ref[0, :, 3 * WIDTH_A:3 * WIDTH_A + WIDTH_B] = (qb * Q_SCALE).astype(qkv_ref.dtype)
    qkv_ref[0, :, 3 * WIDTH_A + WIDTH_B:] = qkv[:, 3 * WIDTH_A + WIDTH_B:].astype(qkv_ref.dtype)

    rows4 = PROJ_ROWS // 4
    rows16 = PROJ_ROWS // 16
    for cb in range(3 * WIDTH_A // LANES):
        cols = slice(cb * LANES, (cb + 1) * LANES)
        val = qkv[:, cols]
        nat, res4 = 2 * (cb % 2), 2 * (cb % 2) + 1
        slab_ref[nat] = val * Q_SCALE if cb < WIDTH_A // LANES else val
        for r4 in range(4):
            part = slab_ref[nat, pl.ds(r4, rows4, stride=4), :]
            a4_ref[0, r4, :, cols] = part.astype(a4_ref.dtype)
            slab_ref[res4, r4 * rows4:(r4 + 1) * rows4, :] = part
        for r4 in range(4):
            for j in range(4):
                part = slab_ref[res4, pl.ds(r4 * rows4 + j, rows16, stride=4), :]
                a16_ref[0, 4 * j + r4, :, cols] = part.astype(a16_ref.dtype)

    @pl.when(t == 0)
    def _():
        kmean_ref[...] = jnp.zeros_like(kmean_ref)

    blocks_per_tile = PROJ_ROWS // MOBA_BLOCK
    for sb in range(blocks_per_tile):
        km = jnp.mean(kb[sb * MOBA_BLOCK:(sb + 1) * MOBA_BLOCK, :], axis=0, keepdims=True)
        kmean_ref[pl.ds(t * blocks_per_tile + sb, 1), :] = km

    kmean = kmean_ref[...]
    lane_head = lax.broadcasted_iota(jnp.int32, kmean.shape, 1) // HEAD_DIM
    wt = jnp.concatenate(
        [jnp.where(lane_head == h, kmean, 0.0) for h in range(N_HEADS_B)], axis=0)
    q_hi, q_lo = _split_bf16(qb)
    w_hi, w_lo = _split_bf16(wt)
    gate_ref[0] = _dot_nt(w_hi, q_hi) + (_dot_nt(w_lo, q_hi) + _dot_nt(w_hi, q_lo))


def _project(x, g_attn, w_in_bf16):
    b, s, d = x.shape
    n_blocks = s // MOBA_BLOCK
    return pl.pallas_call(
        _proj_kernel,
        grid=(b, s // PROJ_ROWS),
        in_specs=[
            pl.BlockSpec((1, PROJ_ROWS, d), lambda i, t: (i, t, 0)),
            pl.BlockSpec((1, d), lambda i, t: (0, 0)),
            pl.BlockSpec((d, 3 * d), lambda i, t: (0, 0)),
        ],
        out_specs=[
            pl.BlockSpec((1, PROJ_ROWS, 3 * d), lambda i, t: (i, t, 0)),
            pl.BlockSpec((1, LANES, PROJ_ROWS), lambda i, t: (i, 0, t)),
            pl.BlockSpec((1, 4, PROJ_ROWS // 4, 3 * WIDTH_A), lambda i, t: (i, 0, t, 0)),
            pl.BlockSpec((1, 16, PROJ_ROWS // 16, 3 * WIDTH_A), lambda i, t: (i, 0, t, 0)),
        ],
        out_shape=[
            jax.ShapeDtypeStruct((b, s, 3 * d), jnp.bfloat16),
            jax.ShapeDtypeStruct((b, LANES, s), jnp.float32),
            jax.ShapeDtypeStruct((b, 4, s // 4, 3 * WIDTH_A), jnp.bfloat16),
            jax.ShapeDtypeStruct((b, 16, s // 16, 3 * WIDTH_A), jnp.bfloat16),
        ],
        scratch_shapes=[pltpu.VMEM((n_blocks, WIDTH_B), jnp.float32),
                        pltpu.VMEM((4, PROJ_ROWS, LANES), jnp.float32)],
        compiler_params=pltpu.CompilerParams(
            dimension_semantics=("arbitrary", "arbitrary"), vmem_limit_bytes=VMEM_LIMIT),
        name="proj",
    )(x, g_attn, w_in_bf16)


def _select_kernel(gate_ref, sel_ref):
    n_blocks = LANES // N_HEADS_B
    width = gate_ref.shape[2]
    q_pos = pl.program_id(1) * width + lax.broadcasted_iota(jnp.int32, (n_blocks, width), 1)
    blk = lax.broadcasted_iota(jnp.int32, (n_blocks, width), 0)
    past = blk < q_pos // MOBA_BLOCK
    for h in range(N_HEADS_B):
        g = jnp.where(past, gate_ref[0, h * n_blocks:(h + 1) * n_blocks, :], NEG_INF)
        rank = jnp.zeros(g.shape, jnp.int32)
        for other_blk in range(n_blocks):
            other = g[other_blk:other_blk + 1, :]
            beats = (other > g) | ((other == g) & (other_blk < blk))
            rank = rank + beats.astype(jnp.int32)
        sel_ref[0, h * n_blocks:(h + 1) * n_blocks, :] = (past & (rank < MOBA_TOPK)).astype(sel_ref.dtype)


def _select(gate):
    b, _, s = gate.shape
    width = 1024
    return pl.pallas_call(
        _select_kernel,
        grid=(b, s // width),
        in_specs=[pl.BlockSpec((1, LANES, width), lambda i, t: (i, 0, t))],
        out_specs=pl.BlockSpec((1, LANES, width), lambda i, t: (i, 0, t)),
        out_shape=jax.ShapeDtypeStruct((b, LANES, s), jnp.float32),
        compiler_params=pltpu.CompilerParams(dimension_semantics=("arbitrary", "arbitrary")),
        name="select",
    )(gate)


def _dilated_kernel(slopes_ref, q1_ref, k1_ref, v1_ref, q4_ref, k4_ref, v4_ref,
                    q16_ref, k16_ref, v16_ref, o_ref,
                    acc_ref, stat_ref, bias_ref, s_ref, p_ref, mloc_ref, stats_ref):
    pair = pl.program_id(0)
    seq = k1_ref.shape[1]
    c = DIL_BLOCK
    n_steps = seq // c
    dilations = tuple(d for _, d in DILATED_PAIRS)

    @pl.when(pl.program_id(1) == 0)
    def _():
        key = lax.broadcasted_iota(jnp.int32, (2 * c, 2 * c), 0)
        lane = lax.broadcasted_iota(jnp.int32, (2 * c, 2 * c), 1)
        slope_l = jnp.where(lane >= c, slopes_ref[2 * pair + 1], slopes_ref[2 * pair])
        for b_idx, dil in enumerate(dilations):
            for first in range(2):
                dist = (0 if first else c) + jnp.bitwise_and(lane, c - 1) - key
                bias_ref[2 * b_idx + first] = jnp.where(
                    (dist >= 0) & (dist <= c), (-dil) * slope_l * dist.astype(jnp.float32), NEG_INF)

    head0 = lax.broadcasted_iota(jnp.int32, (c, LANES), 1) < HEAD_DIM

    branches = ((q1_ref, k1_ref, v1_ref), (q4_ref, k4_ref, v4_ref), (q16_ref, k16_ref, v16_ref))
    for b_idx, (dil, (q_ref, k_ref, v_ref)) in enumerate(zip(dilations, branches)):
        sub_len = seq // dil
        n_chunks = sub_len // c

        def offsets(t, n_chunks=n_chunks, sub_len=sub_len):
            t = jnp.minimum(t, n_steps - 1)
            r = t // n_chunks
            n = t % n_chunks
            q_off = pl.multiple_of(r * sub_len + n * c, c)
            k_off = pl.multiple_of(r * sub_len + jnp.maximum(n - 1, 0) * c, c)
            return r, n, q_off, k_off

        def scores(t, slot, b_idx=b_idx, q_ref=q_ref, k_ref=k_ref, offsets=offsets):
            _, n, q_off, k_off = offsets(t)
            q = q_ref[0, pl.ds(q_off, c), :]
            q_heads = jnp.concatenate([jnp.where(head0, q, jnp.zeros_like(q)),
                                       jnp.where(head0, jnp.zeros_like(q), q)], axis=0)
            keys = k_ref[0, pl.ds(k_off, 2 * c), :]
            first = (n == 0).astype(jnp.int32)
            st = _dot_nt(keys, q_heads) + bias_ref[2 * b_idx + first]
            s_ref[slot] = st
            mloc_ref[slot, 0:1, :] = jnp.max(st, axis=0, keepdims=True)

        def softmax(t, slot):
            m_loc = mloc_ref[slot, 0:1, :]
            p = jnp.exp2(s_ref[slot] - m_loc)
            stats_ref[slot, 0:1, :] = m_loc
            stats_ref[slot, 1:2, :] = jnp.sum(p, axis=0, keepdims=True)
            p_ref[slot] = p.astype(p_ref.dtype)

        def emit(t, slot, b_idx=b_idx, dil=dil, v_ref=v_ref, offsets=offsets):
            r, n, _, k_off = offsets(t)
            vals = v_ref[0, pl.ds(k_off, 2 * c), :]
            pv = lax.dot_general(vals, p_ref[slot], (((0,), (0,)), ((), ())),
                                 preferred_element_type=jnp.float32)
            out_t = jnp.concatenate([pv[:HEAD_DIM, :c], pv[HEAD_DIM:, c:]], axis=0)
            if dil == 1:
                rows = pl.ds(pl.multiple_of(n * c, c), c)
            else:
                rows = pl.ds(r + dil * c * n, c, stride=dil)
            l_loc = stats_ref[slot, 1:2, :]
            inv_l = 1.0 / l_loc
            lse = stats_ref[slot, 0:1, :] + jnp.log2(l_loc)
            out_t = out_t * jnp.concatenate([jnp.broadcast_to(inv_l[:, :c], (HEAD_DIM, c)),
                                             jnp.broadcast_to(inv_l[:, c:], (HEAD_DIM, c))], axis=0)
            lse_t = jnp.concatenate([jnp.broadcast_to(lse[:, :c], (HEAD_DIM, c)),
                                     jnp.broadcast_to(lse[:, c:], (HEAD_DIM, c))], axis=0)
            acc_ref[b_idx, rows, :] = out_t.T
            stat_ref[b_idx, rows, :] = lse_t.T

        _software_pipeline(n_steps, scores, softmax, emit)

    def merge(ci, carry):
        rows = pl.ds(pl.multiple_of(ci * c, c), c)
        lses = [stat_ref[b_idx, rows, :] for b_idx in range(len(dilations))]
        lse_max = functools.reduce(jnp.maximum, lses)
        weights = [jnp.exp2(lse - lse_max) for lse in lses]
        out = functools.reduce(lambda a, b: a + b,
                               [w * acc_ref[b_idx, rows, :] for b_idx, w in enumerate(weights)])
        o_ref[0, rows, :] = (out / functools.reduce(lambda a, b: a + b, weights)).astype(o_ref.dtype)
        return carry

    lax.fori_loop(0, seq // c, merge, 0)


def _dilated(qkv_by_dilation, slopes):
    b, s, _ = qkv_by_dilation[0].shape
    n_pairs = WIDTH_A // LANES
    tok_major = pl.BlockSpec((1, s, LANES), lambda p, i: (i, 0, p))
    operands, in_specs = [slopes], [pl.BlockSpec(memory_space=pltpu.SMEM)]
    for arr in qkv_by_dilation:
        for part in range(3):
            operands.append(arr)
            in_specs.append(pl.BlockSpec((1, s, LANES), lambda p, i, part=part: (i, 0, part * n_pairs + p)))
    n_br = len(DILATED_PAIRS)
    c = DIL_BLOCK
    return pl.pallas_call(
        _dilated_kernel,
        grid=(n_pairs, b),
        in_specs=in_specs,
        out_specs=tok_major,
        out_shape=jax.ShapeDtypeStruct((b, s, WIDTH_A), jnp.float32),
        scratch_shapes=[pltpu.VMEM((n_br, s, LANES), jnp.float32),
                        pltpu.VMEM((n_br, s, LANES), jnp.float32),
                        pltpu.VMEM((2 * n_br, 2 * c, 2 * c), jnp.float32),
                        pltpu.VMEM((PIPE_STEPS, 2 * c, 2 * c), jnp.float32),
                        pltpu.VMEM((PIPE_STEPS, 2 * c, 2 * c), jnp.bfloat16),
                        pltpu.VMEM((PIPE_STEPS, 8, 2 * c), jnp.float32),
                        pltpu.VMEM((PIPE_STEPS, 8, 2 * c), jnp.float32)],
        compiler_params=pltpu.CompilerParams(
            dimension_semantics=("arbitrary", "arbitrary"), vmem_limit_bytes=VMEM_LIMIT),
        name="dilated",
    )(*operands)


MOBA_PAD_STEPS = 2 * PIPE_LAG
M_INIT = -1e30


def _moba_pair_tables(n_blocks):
    pairs = [(i, j) for i in range(n_blocks) for j in range(i + 1)]
    pairs += [pairs[-1]] * MOBA_PAD_STEPS
    return (np.array([p[0] for p in pairs], np.int32), np.array([p[1] for p in pairs], np.int32),
            np.array([int(p[0] == p[1]) for p in pairs], np.int32))


def _moba_kernel(slopes_ref, qblk_ref, kblk_ref, own_ref, q_ref, k_ref, v_ref, sel_ref, o_ref,
                 acc_ref, m_ref, l_ref, bias_ref, s_ref, p_ref, mloc_ref, stats_ref):
    pair = pl.program_id(0)
    bs = MOBA_BLOCK
    n_blocks = k_ref.shape[1] // bs
    n_steps = qblk_ref.shape[0] - MOBA_PAD_STEPS
    sel_rows = LANES // N_HEADS_B
    slopes = (slopes_ref[2 * pair], slopes_ref[2 * pair + 1])

    @pl.when(pl.program_id(1) == 0)
    def _():
        rel = (lax.broadcasted_iota(jnp.int32, (bs, bs), 0)
               - lax.broadcasted_iota(jnp.int32, (bs, bs), 1))
        relf = rel.astype(jnp.float32)
        for h in range(2):
            bias_ref[h] = slopes[h] * relf
            bias_ref[2 + h] = jnp.where(rel <= 0, slopes[h] * relf, NEG_INF)

    m_ref[...] = jnp.full_like(m_ref, M_INIT)
    l_ref[...] = jnp.zeros_like(l_ref)
    acc_ref[...] = jnp.zeros_like(acc_ref)
    head0 = lax.broadcasted_iota(jnp.int32, (bs, LANES), 1) < HEAD_DIM

    def scores(t, slot):
        q0 = pl.multiple_of(qblk_ref[t] * bs, bs)
        k0 = pl.multiple_of(kblk_ref[t] * bs, bs)
        q = q_ref[0, pl.ds(q0, bs), :]
        kj = k_ref[0, pl.ds(k0, bs), :]
        q_heads = (jnp.where(head0, q, jnp.zeros_like(q)), jnp.where(head0, jnp.zeros_like(q), q))
        for h in range(2):
            st = _dot_nt(kj, q_heads[h]) + bias_ref[2 * own_ref[t] + h]
            s_ref[slot, h] = st
            mloc_ref[slot, h:h + 1, :] = jnp.max(st, axis=0, keepdims=True)

    def softmax(t, slot):
        for h in range(2):
            m_loc = mloc_ref[slot, h:h + 1, :]
            p = jnp.exp2(s_ref[slot, h] - m_loc)
            stats_ref[slot, h:h + 1, :] = m_loc
            stats_ref[slot, 2 + h:3 + h, :] = jnp.sum(p, axis=0, keepdims=True)
            p_ref[slot, h] = p.astype(p_ref.dtype)

    def merge(t, slot):
        i = qblk_ref[t]
        j = kblk_ref[t]
        q0 = pl.multiple_of(i * bs, bs)
        k0 = pl.multiple_of(j * bs, bs)
        gap = ((i - j) * bs).astype(jnp.float32)
        is_own = own_ref[t] > 0
        vj = v_ref[0, pl.ds(k0, bs), :]
        for h in range(2):
            pv = lax.dot_general(vj, p_ref[slot, h], (((0,), (0,)), ((), ())),
                                 preferred_element_type=jnp.float32)[h * HEAD_DIM:(h + 1) * HEAD_DIM]
            chosen = (sel_ref[0, pl.ds(h * sel_rows + j, 1), pl.ds(q0, bs)] > 0.5) | is_own
            m_blk = stats_ref[slot, h:h + 1, :] - slopes[h] * gap
            m_old = m_ref[h:h + 1, pl.ds(q0, bs)]
            m_new = jnp.where(chosen, jnp.maximum(m_old, m_blk), m_old)
            a_old = jnp.exp2(m_old - m_new)
            a_blk = jnp.where(chosen, jnp.exp2(m_blk - m_new), 0.0)
            acc_ref[h, :, pl.ds(q0, bs)] = a_old * acc_ref[h, :, pl.ds(q0, bs)] + a_blk * pv
            l_ref[h:h + 1, pl.ds(q0, bs)] = (a_old * l_ref[h:h + 1, pl.ds(q0, bs)]
                                             + a_blk * stats_ref[slot, 2 + h:3 + h, :])
            m_ref[h:h + 1, pl.ds(q0, bs)] = m_new

    _software_pipeline(n_steps, scores, softmax, merge)

    def finish(i, carry):
        q0 = pl.multiple_of(i * bs, bs)
        halves = [acc_ref[h, :, pl.ds(q0, bs)] / l_ref[h:h + 1, pl.ds(q0, bs)] for h in range(2)]
        o_ref[0, pl.ds(q0, bs), :] = jnp.concatenate(halves, axis=0).T.astype(o_ref.dtype)
        return carry

    lax.fori_loop(0, n_blocks, finish, 0)


def _moba(qkv, sel, slopes):
    b, s, _ = qkv.shape
    n_pairs = WIDTH_B // LANES
    q_base = 3 * WIDTH_A // LANES
    sel_rows = 2 * (LANES // N_HEADS_B)
    qblk, kblk, own = _moba_pair_tables(s // MOBA_BLOCK)
    smem = pl.BlockSpec(memory_space=pltpu.SMEM)
    part = lambda n: pl.BlockSpec((1, s, LANES), lambda p, i: (i, 0, q_base + n * n_pairs + p))
    return pl.pallas_call(
        _moba_kernel,
        grid=(n_pairs, b),
        in_specs=[
            smem, smem, smem, smem,
            part(0), part(1), part(2),
            pl.BlockSpec((1, sel_rows, s), lambda p, i: (i, p, 0)),
        ],
        out_specs=pl.BlockSpec((1, s, LANES), lambda p, i: (i, 0, p)),
        out_shape=jax.ShapeDtypeStruct((b, s, WIDTH_B), jnp.float32),
        scratch_shapes=[pltpu.VMEM((2, HEAD_DIM, s), jnp.float32),
                        pltpu.VMEM((8, s), jnp.float32),
                        pltpu.VMEM((8, s), jnp.float32),
                        pltpu.VMEM((4, MOBA_BLOCK, MOBA_BLOCK), jnp.float32),
                        pltpu.VMEM((PIPE_STEPS, 2, MOBA_BLOCK, MOBA_BLOCK), jnp.float32),
                        pltpu.VMEM((PIPE_STEPS, 2, MOBA_BLOCK, MOBA_BLOCK), jnp.bfloat16),
                        pltpu.VMEM((PIPE_STEPS, 8, MOBA_BLOCK), jnp.float32),
                        pltpu.VMEM((PIPE_STEPS, 8, MOBA_BLOCK), jnp.float32)],
        compiler_params=pltpu.CompilerParams(
            dimension_semantics=("arbitrary", "arbitrary"), vmem_limit_bytes=VMEM_LIMIT),
        name="moba",
    )(slopes, jnp.asarray(qblk), jnp.asarray(kblk), jnp.asarray(own), qkv, qkv, qkv, sel)


def _rms(x, g):
    return (x * lax.rsqrt(jnp.mean(x * x, axis=-1, keepdims=True) + EPS)) * g


def _tail_kernel(x_ref, p_ref, oa_ref, ob_ref, ga_ref, gb_ref, wout_ref, gmlp_ref, wup_ref, wdown_ref,
                 gple_ref, wgate_ref, bgate_ref, wproj_ref, gfin_ref, o_ref):
    bf = jnp.bfloat16
    ya = _rms(oa_ref[0].astype(jnp.float32), ga_ref[...]).astype(bf)
    yb = _rms(ob_ref[0].astype(jnp.float32), gb_ref[...]).astype(bf)
    y = (jnp.dot(ya, wout_ref[0:WIDTH_A, :], preferred_element_type=jnp.float32)
         + jnp.dot(yb, wout_ref[WIDTH_A:, :], preferred_element_type=jnp.float32))
    h = x_ref[0] + y

    hn = _rms(h, gmlp_ref[...]).astype(bf)
    ff_chunk = 1024
    for c in range(D_FF // ff_chunk):
        u = jnp.dot(hn, wup_ref[:, c * ff_chunk:(c + 1) * ff_chunk], preferred_element_type=jnp.float32)
        u = jnp.square(jnp.maximum(u, 0.0)).astype(bf)
        h = h + jnp.dot(u, wdown_ref[c * ff_chunk:(c + 1) * ff_chunk, :], preferred_element_type=jnp.float32)

    z = jnp.dot(_rms(h, gple_ref[...]).astype(bf), wgate_ref[...], preferred_element_type=jnp.float32)
    gate = jax.nn.sigmoid(z + bgate_ref[...])
    h = h + gate * jnp.dot(p_ref[0].astype(bf), wproj_ref[...], preferred_element_type=jnp.float32)
    o_ref[0] = _rms(h, gfin_ref[...]).astype(o_ref.dtype)


def _tail(x, p, oa, ob, ga, gb, wout, gmlp, wup, wdown, gple, wgate, bgate, wproj, gfin):
    b, s, d = x.shape
    rows = lambda width: pl.BlockSpec((1, TAIL_ROWS, width), lambda i, t: (i, t, 0))
    full = lambda arr: pl.BlockSpec(arr.shape, lambda i, t: (0,) * arr.ndim,
                                    pipeline_mode=pl.Buffered(1))
    consts = (ga, gb, wout, gmlp, wup, wdown, gple, wgate, bgate, wproj, gfin)
    return pl.pallas_call(
        _tail_kernel,
        grid=(b, s // TAIL_ROWS),
        in_specs=[rows(d), rows(PLE_DIM), rows(WIDTH_A), rows(WIDTH_B)] + [full(a) for a in consts],
        out_specs=rows(d),
        out_shape=jax.ShapeDtypeStruct((b, s, d), x.dtype),
        compiler_params=pltpu.CompilerParams(
            dimension_semantics=("arbitrary", "arbitrary"), vmem_limit_bytes=VMEM_LIMIT),
        name="tail",
    )(x, p, oa, ob, *consts)


def kernel(x, p, g_attn, w_in, g_out_a, g_out_b, w_out, g_mlp, w_up, w_down, g_ple, w_ple_gate,
           b_ple_gate, w_ple_proj, g_final):
    assert x.shape[1:] == (4096, D_MODEL) and p.shape[0] == 1 and w_in.shape[0] == 1
    b, s, _ = x.shape
    bf = jnp.bfloat16
    slopes_a, slopes_b = _alibi_log2_slopes()
    row = lambda v: v.reshape(1, -1)

    qkv, gate, qkv_a4, qkv_a16 = _project(x, row(g_attn[0]), w_in[0].astype(bf))
    sel = _select(gate)
    oa = _dilated([qkv, qkv_a4.reshape(b, s, 3 * WIDTH_A), qkv_a16.reshape(b, s, 3 * WIDTH_A)],
                  jnp.asarray(slopes_a))
    ob = _moba(qkv, sel, jnp.asarray(slopes_b))

    return _tail(x, p[0], oa, ob, row(g_out_a[0]), row(g_out_b[0]), w_out[0].astype(bf), row(g_mlp[0]),
                 w_up[0].astype(bf), w_down[0].astype(bf), row(g_ple[0]), w_ple_gate[0].astype(bf),
                 row(b_ple_gate[0]), w_ple_proj[0].astype(bf), row(g_final))
```

```python
import functools
import math

import jax
import jax.numpy as jnp
import numpy as np
from jax import lax
from jax.experimental import pallas as pl
from jax.experimental.pallas import tpu as pltpu

D_MODEL = 1024
HEAD_DIM = 64
N_HEADS = D_MODEL // HEAD_DIM
N_HEADS_A = N_HEADS // 2
N_HEADS_B = N_HEADS - N_HEADS_A
WIDTH_A = N_HEADS_A * HEAD_DIM
WIDTH_B = N_HEADS_B * HEAD_DIM
DILATED_PAIRS = ((128, 1), (512, 4), (2048, 16))
MOBA_BLOCK = 256
MOBA_TOPK = 3
D_FF = 4 * D_MODEL
PLE_DIM = 256
EPS = 1e-6

LANES = 128
LOG2E = 1.4426950408889634
Q_SCALE = HEAD_DIM ** -0.5 * LOG2E
NEG_INF = float("-inf")
VMEM_LIMIT = 56 * 1024 * 1024

PROJ_ROWS = 512
TAIL_ROWS = 512
DIL_BLOCK = 128


def _alibi_log2_slopes():
    s = np.array([2.0 ** (-8.0 * (i + 1) / N_HEADS) for i in range(N_HEADS)], dtype=np.float64)
    return (s[0::2] * LOG2E).astype(np.float32), (s[1::2] * LOG2E).astype(np.float32)


def _dot_nt(a, b):
    return lax.dot_general(a, b, (((1,), (1,)), ((), ())), preferred_element_type=jnp.float32)


def _split_bf16(x):
    hi = x.astype(jnp.bfloat16)
    lo = (x - hi.astype(jnp.float32)).astype(jnp.bfloat16)
    return hi, lo


PIPE_LAG = 2
PIPE_STEPS = 2 * PIPE_LAG


def _software_pipeline(n_steps, scores, softmax, emit, scores_first=True):
    assert n_steps % PIPE_STEPS == 0
    for t in range(2 * PIPE_LAG):
        scores(t, t % PIPE_STEPS)
    for t in range(PIPE_LAG):
        softmax(t, t % PIPE_STEPS)

    def body(it, carry):
        for k in range(PIPE_STEPS):
            t = PIPE_STEPS * it + k
            if scores_first:
                scores(t + 2 * PIPE_LAG, k)
            softmax(t + PIPE_LAG, (k + PIPE_LAG) % PIPE_STEPS)
            emit(t, k)
            if not scores_first:
                scores(t + 2 * PIPE_LAG, k)
        return carry

    lax.fori_loop(0, n_steps // PIPE_STEPS, body, 0)


def _moba_slope_split():
    _, slopes = _alibi_log2_slopes()
    hi = slopes.astype(jnp.bfloat16).astype(np.float32)
    lo = (slopes - hi).astype(jnp.bfloat16).astype(np.float32)
    return hi, lo


def _proj_kernel(x_ref, g_ref, w_ref, qkv_ref, gate_ref, a4_ref, a16_ref, moba_ref, kmean_ref, slab_ref,
                 *, slope_hi, slope_lo):
    t = pl.program_id(1)
    x = x_ref[0]
    ms = jnp.mean(x * x, axis=-1, keepdims=True)
    hn = (x * lax.rsqrt(ms + EPS)) * g_ref[...]
    qkv = jnp.dot(hn.astype(jnp.bfloat16), w_ref[...], preferred_element_type=jnp.float32)

    qa = qkv[:, 0:WIDTH_A]
    qb = qkv[:, 3 * WIDTH_A:3 * WIDTH_A + WIDTH_B]
    kb = qkv[:, 3 * WIDTH_A + WIDTH_B:3 * WIDTH_A + 2 * WIDTH_B]
    vb = qkv[:, 3 * WIDTH_A + 2 * WIDTH_B:]
    qkv_ref[0, :, 0:WIDTH_A] = (qa * Q_SCALE).astype(qkv_ref.dtype)
    qkv_ref[0, :, WIDTH_A:] = qkv[:, WIDTH_A:3 * WIDTH_A].astype(qkv_ref.dtype)

    lane = lax.broadcasted_iota(jnp.int32, (PROJ_ROWS, LANES), 1)
    key_pos = jnp.bitwise_and(lax.broadcasted_iota(jnp.int32, (PROJ_ROWS, LANES), 0),
                              MOBA_BLOCK - 1).astype(jnp.float32)
    for pair in range(WIDTH_B // LANES):
        cols = slice(pair * LANES, (pair + 1) * LANES)
        q_t, k_t, v_t = qb[:, cols] * Q_SCALE, kb[:, cols], vb[:, cols]
        for h in range(2):
            own = (lane < HEAD_DIM) if h == 0 else (lane >= HEAD_DIM)
            f0 = HEAD_DIM if h == 0 else 0
            head = 2 * pair + h
            q_feat = jnp.where(lane == f0, float(slope_hi[head]),
                               jnp.where(lane == f0 + 1, float(slope_lo[head]), 0.0))
            k_feat = jnp.where((lane == f0) | (lane == f0 + 1), key_pos, 0.0)
            base = (pair * 6 + h) * LANES
            for n, tile in enumerate((jnp.where(own, q_t, q_feat), jnp.where(own, k_t, k_feat),
                                      jnp.where(own, v_t, 1.0))):
                moba_ref[0, :, base + 2 * n * LANES:base + (2 * n + 1) * LANES] = tile.astype(moba_ref.dtype)

    rows4 = PROJ_ROWS // 4
    rows16 = PROJ_ROWS // 16
    for cb in range(3 * WIDTH_A // LANES):
        cols = slice(cb * LANES, (cb + 1) * LANES)
        val = qkv[:, cols]
        nat, res4 = 2 * (cb % 2), 2 * (cb % 2) + 1
        slab_ref[nat] = val * Q_SCALE if cb < WIDTH_A // LANES else val
        for r4 in range(4):
            part = slab_ref[nat, pl.ds(r4, rows4, stride=4), :]
            a4_ref[0, r4, :, cols] = part.astype(a4_ref.dtype)
            slab_ref[res4, r4 * rows4:(r4 + 1) * rows4, :] = part
        for r4 in range(4):
            for j in range(4):
                part = slab_ref[res4, pl.ds(r4 * rows4 + j, rows16, stride=4), :]
                a16_ref[0, 4 * j + r4, :, cols] = part.astype(a16_ref.dtype)

    @pl.when(t == 0)
    def _():
        kmean_ref[...] = jnp.zeros_like(kmean_ref)

    blocks_per_tile = PROJ_ROWS // MOBA_BLOCK
    for sb in range(blocks_per_tile):
        km = jnp.mean(kb[sb * MOBA_BLOCK:(sb + 1) * MOBA_BLOCK, :], axis=0, keepdims=True)
        kmean_ref[pl.ds(t * blocks_per_tile + sb, 1), :] = km

    kmean = kmean_ref[...]
    lane_head = lax.broadcasted_iota(jnp.int32, kmean.shape, 1) // HEAD_DIM
    wt = jnp.concatenate(
        [jnp.where(lane_head == h, kmean, 0.0) for h in range(N_HEADS_B)], axis=0)
    q_hi, q_lo = _split_bf16(qb)
    w_hi, w_lo = _split_bf16(wt)
    gate_ref[0] = _dot_nt(w_hi, q_hi) + (_dot_nt(w_lo, q_hi) + _dot_nt(w_hi, q_lo))


def _project(x, g_attn, w_in_bf16):
    b, s, d = x.shape
    n_blocks = s // MOBA_BLOCK
    slope_hi, slope_lo = _moba_slope_split()
    return pl.pallas_call(
        functools.partial(_proj_kernel, slope_hi=slope_hi, slope_lo=slope_lo),
        grid=(b, s // PROJ_ROWS),
        in_specs=[
            pl.BlockSpec((1, PROJ_ROWS, d), lambda i, t: (i, t, 0)),
            pl.BlockSpec((1, d), lambda i, t: (0, 0)),
            pl.BlockSpec((d, 3 * d), lambda i, t: (0, 0)),
        ],
        out_specs=[
            pl.BlockSpec((1, PROJ_ROWS, 3 * WIDTH_A), lambda i, t: (i, t, 0)),
            pl.BlockSpec((1, LANES, PROJ_ROWS), lambda i, t: (i, 0, t)),
            pl.BlockSpec((1, 4, PROJ_ROWS // 4, 3 * WIDTH_A), lambda i, t: (i, 0, t, 0)),
            pl.BlockSpec((1, 16, PROJ_ROWS // 16, 3 * WIDTH_A), lambda i, t: (i, 0, t, 0)),
            pl.BlockSpec((1, PROJ_ROWS, 6 * WIDTH_B), lambda i, t: (i, t, 0)),
        ],
        out_shape=[
            jax.ShapeDtypeStruct((b, s, 3 * WIDTH_A), jnp.bfloat16),
            jax.ShapeDtypeStruct((b, LANES, s), jnp.float32),
            jax.ShapeDtypeStruct((b, 4, s // 4, 3 * WIDTH_A), jnp.bfloat16),
            jax.ShapeDtypeStruct((b, 16, s // 16, 3 * WIDTH_A), jnp.bfloat16),
            jax.ShapeDtypeStruct((b, s, 6 * WIDTH_B), jnp.bfloat16),
        ],
        scratch_shapes=[pltpu.VMEM((n_blocks, WIDTH_B), jnp.float32),
                        pltpu.VMEM((4, PROJ_ROWS, LANES), jnp.float32)],
        compiler_params=pltpu.CompilerParams(
            dimension_semantics=("arbitrary", "arbitrary"), vmem_limit_bytes=VMEM_LIMIT),
        name="proj",
    )(x, g_attn, w_in_bf16)


def _select_kernel(gate_ref, sel_ref):
    n_blocks = LANES // N_HEADS_B
    width = gate_ref.shape[2]
    q_pos = pl.program_id(1) * width + lax.broadcasted_iota(jnp.int32, (n_blocks, width), 1)
    blk = lax.broadcasted_iota(jnp.int32, (n_blocks, width), 0)
    past = blk < q_pos // MOBA_BLOCK
    for h in range(N_HEADS_B):
        g = jnp.where(past, gate_ref[0, h * n_blocks:(h + 1) * n_blocks, :], NEG_INF)
        rank = jnp.zeros(g.shape, jnp.int32)
        for other_blk in range(n_blocks):
            other = g[other_blk:other_blk + 1, :]
            beats = (other > g) | ((other == g) & (other_blk < blk))
            rank = rank + beats.astype(jnp.int32)
        sel_ref[0, h * n_blocks:(h + 1) * n_blocks, :] = (past & (rank < MOBA_TOPK)).astype(sel_ref.dtype)


def _select(gate):
    b, _, s = gate.shape
    width = 1024
    return pl.pallas_call(
        _select_kernel,
        grid=(b, s // width),
        in_specs=[pl.BlockSpec((1, LANES, width), lambda i, t: (i, 0, t))],
        out_specs=pl.BlockSpec((1, LANES, width), lambda i, t: (i, 0, t)),
        out_shape=jax.ShapeDtypeStruct((b, LANES, s), jnp.float32),
        compiler_params=pltpu.CompilerParams(dimension_semantics=("arbitrary", "arbitrary")),
        name="select",
    )(gate)


def _dilated_kernel(slopes_ref, q1_ref, k1_ref, v1_ref, q4_ref, k4_ref, v4_ref,
                    q16_ref, k16_ref, v16_ref, o_ref,
                    acc_ref, stat_ref, bias_ref, s_ref, p_ref, mloc_ref, stats_ref):
    pair = pl.program_id(0)
    seq = k1_ref.shape[1]
    c = DIL_BLOCK
    n_steps = seq // c
    dilations = tuple(d for _, d in DILATED_PAIRS)

    @pl.when(pl.program_id(1) == 0)
    def _():
        key = lax.broadcasted_iota(jnp.int32, (2 * c, 2 * c), 0)
        lane = lax.broadcasted_iota(jnp.int32, (2 * c, 2 * c), 1)
        slope_l = jnp.where(lane >= c, slopes_ref[2 * pair + 1], slopes_ref[2 * pair])
        for b_idx, dil in enumerate(dilations):
            for first in range(2):
                dist = (0 if first else c) + jnp.bitwise_and(lane, c - 1) - key
                bias_ref[2 * b_idx + first] = jnp.where(
                    (dist >= 0) & (dist <= c), (-dil) * slope_l * dist.astype(jnp.float32), NEG_INF)

    head0 = lax.broadcasted_iota(jnp.int32, (c, LANES), 1) < HEAD_DIM

    branches = ((q1_ref, k1_ref, v1_ref), (q4_ref, k4_ref, v4_ref), (q16_ref, k16_ref, v16_ref))
    for b_idx, (dil, (q_ref, k_ref, v_ref)) in enumerate(zip(dilations, branches)):
        sub_len = seq // dil
        n_chunks = sub_len // c

        def offsets(t, n_chunks=n_chunks, sub_len=sub_len):
            t = jnp.minimum(t, n_steps - 1)
            r = t // n_chunks
            n = t % n_chunks
            q_off = pl.multiple_of(r * sub_len + n * c, c)
            k_off = pl.multiple_of(r * sub_len + jnp.maximum(n - 1, 0) * c, c)
            return r, n, q_off, k_off

        def scores(t, slot, b_idx=b_idx, q_ref=q_ref, k_ref=k_ref, offsets=offsets):
            _, n, q_off, k_off = offsets(t)
            q = q_ref[0, pl.ds(q_off, c), :]
            q_heads = jnp.concatenate([jnp.where(head0, q, jnp.zeros_like(q)),
                                       jnp.where(head0, jnp.zeros_like(q), q)], axis=0)
            keys = k_ref[0, pl.ds(k_off, 2 * c), :]
            first = (n == 0).astype(jnp.int32)
            st = _dot_nt(keys, q_heads) + bias_ref[2 * b_idx + first]
            s_ref[slot] = st
            mloc_ref[slot, 0:1, :] = jnp.max(st, axis=0, keepdims=True)

        def softmax(t, slot):
            m_loc = mloc_ref[slot, 0:1, :]
            p = jnp.exp2(s_ref[slot] - m_loc)
            stats_ref[slot, 0:1, :] = m_loc
            stats_ref[slot, 1:2, :] = jnp.sum(p, axis=0, keepdims=True)
            p_ref[slot] = p.astype(p_ref.dtype)

        def emit(t, slot, b_idx=b_idx, dil=dil, v_ref=v_ref, offsets=offsets):
            r, n, _, k_off = offsets(t)
            vals = v_ref[0, pl.ds(k_off, 2 * c), :]
            pv = lax.dot_general(vals, p_ref[slot], (((0,), (0,)), ((), ())),
                                 preferred_element_type=jnp.float32)
            out_t = jnp.concatenate([pv[:HEAD_DIM, :c], pv[HEAD_DIM:, c:]], axis=0)
            if dil == 1:
                rows = pl.ds(pl.multiple_of(n * c, c), c)
            else:
                rows = pl.ds(r + dil * c * n, c, stride=dil)
            l_loc = stats_ref[slot, 1:2, :]
            inv_l = 1.0 / l_loc
            lse = stats_ref[slot, 0:1, :] + jnp.log2(l_loc)
            out_t = out_t * jnp.concatenate([jnp.broadcast_to(inv_l[:, :c], (HEAD_DIM, c)),
                                             jnp.broadcast_to(inv_l[:, c:], (HEAD_DIM, c))], axis=0)
            lse_t = jnp.concatenate([jnp.broadcast_to(lse[:, :c], (HEAD_DIM, c)),
                                     jnp.broadcast_to(lse[:, c:], (HEAD_DIM, c))], axis=0)
            acc_ref[b_idx, rows, :] = out_t.T
            stat_ref[b_idx, rows, :] = lse_t.T

        _software_pipeline(n_steps, scores, softmax, emit)

    def merge(ci, carry):
        rows = pl.ds(pl.multiple_of(ci * c, c), c)
        lses = [stat_ref[b_idx, rows, :] for b_idx in range(len(dilations))]
        lse_max = functools.reduce(jnp.maximum, lses)
        weights = [jnp.exp2(lse - lse_max) for lse in lses]
        out = functools.reduce(lambda a, b: a + b,
                               [w * acc_ref[b_idx, rows, :] for b_idx, w in enumerate(weights)])
        o_ref[0, rows, :] = (out / functools.reduce(lambda a, b: a + b, weights)).astype(o_ref.dtype)
        return carry

    lax.fori_loop(0, seq // c, merge, 0)


def _dilated(qkv_by_dilation, slopes):
    b, s, _ = qkv_by_dilation[0].shape
    n_pairs = WIDTH_A // LANES
    tok_major = pl.BlockSpec((1, s, LANES), lambda p, i: (i, 0, p))
    operands, in_specs = [slopes], [pl.BlockSpec(memory_space=pltpu.SMEM)]
    for arr in qkv_by_dilation:
        for part in range(3):
            operands.append(arr)
            in_specs.append(pl.BlockSpec((1, s, LANES), lambda p, i, part=part: (i, 0, part * n_pairs + p)))
    n_br = len(DILATED_PAIRS)
    c = DIL_BLOCK
    return pl.pallas_call(
        _dilated_kernel,
        grid=(n_pairs, b),
        in_specs=in_specs,
        out_specs=tok_major,
        out_shape=jax.ShapeDtypeStruct((b, s, WIDTH_A), jnp.float32),
        scratch_shapes=[pltpu.VMEM((n_br, s, LANES), jnp.float32),
                        pltpu.VMEM((n_br, s, LANES), jnp.float32),
                        pltpu.VMEM((2 * n_br, 2 * c, 2 * c), jnp.float32),
                        pltpu.VMEM((PIPE_STEPS, 2 * c, 2 * c), jnp.float32),
                        pltpu.VMEM((PIPE_STEPS, 2 * c, 2 * c), jnp.bfloat16),
                        pltpu.VMEM((PIPE_STEPS, 8, 2 * c), jnp.float32),
                        pltpu.VMEM((PIPE_STEPS, 8, 2 * c), jnp.float32)],
        compiler_params=pltpu.CompilerParams(
            dimension_semantics=("arbitrary", "arbitrary"), vmem_limit_bytes=VMEM_LIMIT),
        name="dilated",
    )(*operands)


MOBA_PAD_STEPS = 2 * PIPE_LAG
M_INIT = -1e30


def _moba_pair_tables(n_blocks):
    pairs = [(i, j) for i in range(1, n_blocks) for j in range(i)]
    pairs += [pairs[-1]] * MOBA_PAD_STEPS
    return np.array([p[0] for p in pairs], np.int32), np.array([p[1] for p in pairs], np.int32)


def _moba_kernel(slopes_ref, qblk_ref, kblk_ref, q0_ref, q1_ref, k0_ref, k1_ref, v0_ref, v1_ref, sel_ref,
                 o_ref, acc_ref, m_ref, l_ref, mask_ref, *rings):
    s_ref, p_ref, mloc_ref, stats_ref = (rings[n * PIPE_STEPS:(n + 1) * PIPE_STEPS] for n in range(4))
    pair = pl.program_id(0)
    bs = MOBA_BLOCK
    n_blocks = k0_ref.shape[1] // bs
    n_past = qblk_ref.shape[0] - MOBA_PAD_STEPS
    sel_rows = LANES // N_HEADS_B
    slopes = (slopes_ref[2 * pair], slopes_ref[2 * pair + 1])
    q_refs, k_refs, v_refs = (q0_ref, q1_ref), (k0_ref, k1_ref), (v0_ref, v1_ref)
    sum_row = (HEAD_DIM, 0)

    @pl.when(pl.program_id(1) == 0)
    def _():
        rel = (lax.broadcasted_iota(jnp.int32, (bs, bs), 0)
               - lax.broadcasted_iota(jnp.int32, (bs, bs), 1))
        mask_ref[...] = jnp.where(rel <= 0, 0.0, NEG_INF)

    m_ref[...] = jnp.full_like(m_ref, M_INIT)
    l_ref[...] = jnp.zeros_like(l_ref)
    acc_ref[...] = jnp.zeros_like(acc_ref)
    q_pos = lax.broadcasted_iota(jnp.int32, (1, bs), 1).astype(jnp.float32)

    def make_stages(blocks_of, own):
        def scores(t, slot):
            i, j = blocks_of(t)
            q_off = pl.multiple_of(i * bs, bs)
            k_off = pl.multiple_of(j * bs, bs)
            for h in range(2):
                st = _dot_nt(k_refs[h][0, pl.ds(k_off, bs), :], q_refs[h][0, pl.ds(q_off, bs), :])
                if own:
                    st = st + mask_ref[...]
                s_ref[slot][h] = st
                mloc_ref[slot][h:h + 1, :] = jnp.max(st, axis=0, keepdims=True)

        def softmax(t, slot):
            for h in range(2):
                m_loc = mloc_ref[slot][h:h + 1, :]
                stats_ref[slot][h:h + 1, :] = m_loc
                p_ref[slot][h] = jnp.exp2(s_ref[slot][h] - m_loc).astype(p_ref[slot].dtype)

        def merge(t, slot):
            i, j = blocks_of(t)
            q_off = pl.multiple_of(i * bs, bs)
            k_off = pl.multiple_of(j * bs, bs)
            gap = ((i - j) * bs).astype(jnp.float32)
            for h in range(2):
                pv = lax.dot_general(v_refs[h][0, pl.ds(k_off, bs), :], p_ref[slot][h],
                                     (((0,), (0,)), ((), ())), preferred_element_type=jnp.float32)
                out_t = pv[h * HEAD_DIM:(h + 1) * HEAD_DIM]
                l_blk = pv[sum_row[h]:sum_row[h] + 1]
                m_blk = stats_ref[slot][h:h + 1, :] - slopes[h] * (q_pos + gap)
                m_old = m_ref[h:h + 1, pl.ds(q_off, bs)]
                if own:
                    m_new = jnp.maximum(m_old, m_blk)
                    a_blk = jnp.exp2(m_blk - m_new)
                else:
                    chosen = sel_ref[0, pl.ds(h * sel_rows + j, 1), pl.ds(q_off, bs)] > 0.5
                    m_new = jnp.where(chosen, jnp.maximum(m_old, m_blk), m_old)
                    a_blk = jnp.where(chosen, jnp.exp2(m_blk - m_new), 0.0)
                a_old = jnp.exp2(m_old - m_new)
                acc_ref[h, :, pl.ds(q_off, bs)] = a_old * acc_ref[h, :, pl.ds(q_off, bs)] + a_blk * out_t
                l_ref[h:h + 1, pl.ds(q_off, bs)] = a_old * l_ref[h:h + 1, pl.ds(q_off, bs)] + a_blk * l_blk
                m_ref[h:h + 1, pl.ds(q_off, bs)] = m_new

        return scores, softmax, merge

    def own_blocks(t):
        i = jnp.minimum(t, n_blocks - 1)
        return i, i

    _software_pipeline(n_blocks, *make_stages(own_blocks, own=True), scores_first=False)
    _software_pipeline(n_past, *make_stages(lambda t: (qblk_ref[t], kblk_ref[t]), own=False),
                       scores_first=False)

    def finish(i, carry):
        q0 = pl.multiple_of(i * bs, bs)
        halves = [acc_ref[h, :, pl.ds(q0, bs)] / l_ref[h:h + 1, pl.ds(q0, bs)] for h in range(2)]
        o_ref[0, pl.ds(q0, bs), :] = jnp.concatenate(halves, axis=0).T.astype(o_ref.dtype)
        return carry

    lax.fori_loop(0, n_blocks, finish, 0)


def _moba(heads, sel, slopes):
    b, s, _ = heads.shape
    n_pairs = WIDTH_B // LANES
    sel_rows = 2 * (LANES // N_HEADS_B)
    qblk, kblk = _moba_pair_tables(s // MOBA_BLOCK)
    smem = pl.BlockSpec(memory_space=pltpu.SMEM)
    tile = lambda n: pl.BlockSpec((1, s, LANES), lambda p, i: (i, 0, 6 * p + n))
    return pl.pallas_call(
        _moba_kernel,
        grid=(n_pairs, b),
        in_specs=[smem, smem, smem] + [tile(n) for n in range(6)]
                 + [pl.BlockSpec((1, sel_rows, s), lambda p, i: (i, p, 0))],
        out_specs=pl.BlockSpec((1, s, LANES), lambda p, i: (i, 0, p)),
        out_shape=jax.ShapeDtypeStruct((b, s, WIDTH_B), jnp.float32),
        scratch_shapes=[pltpu.VMEM((2, HEAD_DIM, s), jnp.float32),
                        pltpu.VMEM((8, s), jnp.float32),
                        pltpu.VMEM((8, s), jnp.float32),
                        pltpu.VMEM((MOBA_BLOCK, MOBA_BLOCK), jnp.float32)]
                       + [pltpu.VMEM((2, MOBA_BLOCK, MOBA_BLOCK), jnp.float32)] * PIPE_STEPS
                       + [pltpu.VMEM((2, MOBA_BLOCK, MOBA_BLOCK), jnp.bfloat16)] * PIPE_STEPS
                       + [pltpu.VMEM((8, MOBA_BLOCK), jnp.float32)] * (2 * PIPE_STEPS),
        compiler_params=pltpu.CompilerParams(
            dimension_semantics=("arbitrary", "arbitrary"), vmem_limit_bytes=VMEM_LIMIT),
        name="moba",
    )(slopes, jnp.asarray(qblk), jnp.asarray(kblk), *([heads] * 6), sel)


def _rms(x, g):
    return (x * lax.rsqrt(jnp.mean(x * x, axis=-1, keepdims=True) + EPS)) * g


def _tail_kernel(x_ref, p_ref, oa_ref, ob_ref, ga_ref, gb_ref, wout_ref, gmlp_ref, wup_ref, wdown_ref,
                 gple_ref, wgate_ref, bgate_ref, wproj_ref, gfin_ref, o_ref):
    bf = jnp.bfloat16
    ya = _rms(oa_ref[0].astype(jnp.float32), ga_ref[...]).astype(bf)
    yb = _rms(ob_ref[0].astype(jnp.float32), gb_ref[...]).astype(bf)
    y = (jnp.dot(ya, wout_ref[0:WIDTH_A, :], preferred_element_type=jnp.float32)
         + jnp.dot(yb, wout_ref[WIDTH_A:, :], preferred_element_type=jnp.float32))
    h = x_ref[0] + y

    hn = _rms(h, gmlp_ref[...]).astype(bf)
    ff_chunk = 1024
    for c in range(D_FF // ff_chunk):
        u = jnp.dot(hn, wup_ref[:, c * ff_chunk:(c + 1) * ff_chunk], preferred_element_type=jnp.float32)
        u = jnp.square(jnp.maximum(u, 0.0)).astype(bf)
        h = h + jnp.dot(u, wdown_ref[c * ff_chunk:(c + 1) * ff_chunk, :], preferred_element_type=jnp.float32)

    z = jnp.dot(_rms(h, gple_ref[...]).astype(bf), wgate_ref[...], preferred_element_type=jnp.float32)
    gate = jax.nn.sigmoid(z + bgate_ref[...])
    h = h + gate * jnp.dot(p_ref[0].astype(bf), wproj_ref[...], preferred_element_type=jnp.float32)
    o_ref[0] = _rms(h, gfin_ref[...]).astype(o_ref.dtype)


def _tail(x, p, oa, ob, ga, gb, wout, gmlp, wup, wdown, gple, wgate, bgate, wproj, gfin):
    b, s, d = x.shape
    rows = lambda width: pl.BlockSpec((1, TAIL_ROWS, width), lambda i, t: (i, t, 0))
    full = lambda arr: pl.BlockSpec(arr.shape, lambda i, t: (0,) * arr.ndim,
                                    pipeline_mode=pl.Buffered(1))
    consts = (ga, gb, wout, gmlp, wup, wdown, gple, wgate, bgate, wproj, gfin)
    return pl.pallas_call(
        _tail_kernel,
        grid=(b, s // TAIL_ROWS),
        in_specs=[rows(d), rows(PLE_DIM), rows(WIDTH_A), rows(WIDTH_B)] + [full(a) for a in consts],
        out_specs=rows(d),
        out_shape=jax.ShapeDtypeStruct((b, s, d), x.dtype),
        compiler_params=pltpu.CompilerParams(
            dimension_semantics=("arbitrary", "arbitrary"), vmem_limit_bytes=VMEM_LIMIT),
        name="tail",
    )(x, p, oa, ob, *consts)


def kernel(x, p, g_attn, w_in, g_out_a, g_out_b, w_out, g_mlp, w_up, w_down, g_ple, w_ple_gate,
           b_ple_gate, w_ple_proj, g_final):
    assert x.shape[1:] == (4096, D_MODEL) and p.shape[0] == 1 and w_in.shape[0] == 1
    b, s, _ = x.shape
    bf = jnp.bfloat16
    slopes_a, slopes_b = _alibi_log2_slopes()
    row = lambda v: v.reshape(1, -1)

    qkv_a, gate, qkv_a4, qkv_a16, moba_heads = _project(x, row(g_attn[0]), w_in[0].astype(bf))
    sel = _select(gate)
    oa = _dilated([qkv_a, qkv_a4.reshape(b, s, 3 * WIDTH_A), qkv_a16.reshape(b, s, 3 * WIDTH_A)],
                  jnp.asarray(slopes_a))
    ob = _moba(moba_heads, sel, jnp.asarray(slopes_b))

    return _tail(x, p[0], oa, ob, row(g_out_a[0]), row(g_out_b[0]), w_out[0].astype(bf), row(g_mlp[0]),
                 w_up[0].astype(bf), w_down[0].astype(bf), row(g_ple[0]), w_ple_gate[0].astype(bf),
                 row(b_ple_gate[0]), w_ple_proj[0].astype(bf), row(g_final))
```

```python
import functools
import math

import jax
import jax.numpy as jnp
import numpy as np
from jax import lax
from jax.experimental import pallas as pl
from jax.experimental.pallas import tpu as pltpu

D_MODEL = 1024
HEAD_DIM = 64
N_HEADS = D_MODEL // HEAD_DIM
N_HEADS_A = N_HEADS // 2
N_HEADS_B = N_HEADS - N_HEADS_A
WIDTH_A = N_HEADS_A * HEAD_DIM
WIDTH_B = N_HEADS_B * HEAD_DIM
DILATED_PAIRS = ((128, 1), (512, 4), (2048, 16))
MOBA_BLOCK = 256
MOBA_TOPK = 3
D_FF = 4 * D_MODEL
PLE_DIM = 256
EPS = 1e-6

LANES = 128
LOG2E = 1.4426950408889634
Q_SCALE = HEAD_DIM ** -0.5 * LOG2E
NEG_INF = float("-inf")
VMEM_LIMIT = 56 * 1024 * 1024

PROJ_ROWS = 512
TAIL_ROWS = 512
DIL_BLOCK = 128


def _alibi_log2_slopes():
    s = np.array([2.0 ** (-8.0 * (i + 1) / N_HEADS) for i in range(N_HEADS)], dtype=np.float64)
    return (s[0::2] * LOG2E).astype(np.float32), (s[1::2] * LOG2E).astype(np.float32)


def _dot_nt(a, b):
    return lax.dot_general(a, b, (((1,), (1,)), ((), ())), preferred_element_type=jnp.float32)


def _split_bf16(x):
    hi = x.astype(jnp.bfloat16)
    lo = (x - hi.astype(jnp.float32)).astype(jnp.bfloat16)
    return hi, lo


DIL_LAG = 2
MOBA_LAG = 4
MOBA_RING = 2 * MOBA_LAG


def _software_pipeline(n_steps, scores, softmax, emit, lag, scores_first=True):
    ring = 2 * lag
    assert n_steps % ring == 0
    for t in range(ring):
        scores(t, t)
    for t in range(lag):
        softmax(t, t)

    def body(it, carry):
        for k in range(ring):
            t = ring * it + k
            if scores_first:
                scores(t + ring, k)
            softmax(t + lag, (k + lag) % ring)
            emit(t, k)
            if not scores_first:
                scores(t + ring, k)
        return carry

    lax.fori_loop(0, n_steps // ring, body, 0)


def _moba_slope_split():
    _, slopes = _alibi_log2_slopes()
    hi = slopes.astype(jnp.bfloat16).astype(np.float32)
    lo = (slopes - hi).astype(jnp.bfloat16).astype(np.float32)
    return hi, lo


def _proj_kernel(x_ref, g_ref, w_ref, qkv_ref, gate_ref, a4_ref, a16_ref, moba_ref, kmean_ref, slab_ref,
                 *, slope_hi, slope_lo):
    t = pl.program_id(1)
    x = x_ref[0]
    ms = jnp.mean(x * x, axis=-1, keepdims=True)
    hn = (x * lax.rsqrt(ms + EPS)) * g_ref[...]
    qkv = jnp.dot(hn.astype(jnp.bfloat16), w_ref[...], preferred_element_type=jnp.float32)

    qa = qkv[:, 0:WIDTH_A]
    qb = qkv[:, 3 * WIDTH_A:3 * WIDTH_A + WIDTH_B]
    kb = qkv[:, 3 * WIDTH_A + WIDTH_B:3 * WIDTH_A + 2 * WIDTH_B]
    vb = qkv[:, 3 * WIDTH_A + 2 * WIDTH_B:]
    qkv_ref[0, :, 0:WIDTH_A] = (qa * Q_SCALE).astype(qkv_ref.dtype)
    qkv_ref[0, :, WIDTH_A:] = qkv[:, WIDTH_A:3 * WIDTH_A].astype(qkv_ref.dtype)

    lane = lax.broadcasted_iota(jnp.int32, (PROJ_ROWS, LANES), 1)
    key_pos = jnp.bitwise_and(lax.broadcasted_iota(jnp.int32, (PROJ_ROWS, LANES), 0),
                              MOBA_BLOCK - 1).astype(jnp.float32)
    for pair in range(WIDTH_B // LANES):
        cols = slice(pair * LANES, (pair + 1) * LANES)
        q_t, k_t, v_t = qb[:, cols] * Q_SCALE, kb[:, cols], vb[:, cols]
        for h in range(2):
            own = (lane < HEAD_DIM) if h == 0 else (lane >= HEAD_DIM)
            f0 = HEAD_DIM if h == 0 else 0
            head = 2 * pair + h
            q_feat = jnp.where(lane == f0, float(slope_hi[head]),
                               jnp.where(lane == f0 + 1, float(slope_lo[head]), 0.0))
            k_feat = jnp.where((lane == f0) | (lane == f0 + 1), key_pos, 0.0)
            base = (pair * 6 + h) * LANES
            for n, tile in enumerate((jnp.where(own, q_t, q_feat), jnp.where(own, k_t, k_feat),
                                      jnp.where(own, v_t, 1.0))):
                moba_ref[0, :, base + 2 * n * LANES:base + (2 * n + 1) * LANES] = tile.astype(moba_ref.dtype)

    rows4 = PROJ_ROWS // 4
    rows16 = PROJ_ROWS // 16
    for cb in range(3 * WIDTH_A // LANES):
        cols = slice(cb * LANES, (cb + 1) * LANES)
        val = qkv[:, cols]
        nat, res4 = 2 * (cb % 2), 2 * (cb % 2) + 1
        slab_ref[nat] = val * Q_SCALE if cb < WIDTH_A // LANES else val
        for r4 in range(4):
            part = slab_ref[nat, pl.ds(r4, rows4, stride=4), :]
            a4_ref[0, r4, :, cols] = part.astype(a4_ref.dtype)
            slab_ref[res4, r4 * rows4:(r4 + 1) * rows4, :] = part
        for r4 in range(4):
            for j in range(4):
                part = slab_ref[res4, pl.ds(r4 * rows4 + j, rows16, stride=4), :]
                a16_ref[0, 4 * j + r4, :, cols] = part.astype(a16_ref.dtype)

    @pl.when(t == 0)
    def _():
        kmean_ref[...] = jnp.zeros_like(kmean_ref)

    blocks_per_tile = PROJ_ROWS // MOBA_BLOCK
    for sb in range(blocks_per_tile):
        km = jnp.mean(kb[sb * MOBA_BLOCK:(sb + 1) * MOBA_BLOCK, :], axis=0, keepdims=True)
        kmean_ref[pl.ds(t * blocks_per_tile + sb, 1), :] = km

    kmean = kmean_ref[...]
    lane_head = lax.broadcasted_iota(jnp.int32, kmean.shape, 1) // HEAD_DIM
    wt = jnp.concatenate(
        [jnp.where(lane_head == h, kmean, 0.0) for h in range(N_HEADS_B)], axis=0)
    q_hi, q_lo = _split_bf16(qb)
    w_hi, w_lo = _split_bf16(wt)
    gate_ref[0] = _dot_nt(w_hi, q_hi) + (_dot_nt(w_lo, q_hi) + _dot_nt(w_hi, q_lo))


def _project(x, g_attn, w_in_bf16):
    b, s, d = x.shape
    n_blocks = s // MOBA_BLOCK
    slope_hi, slope_lo = _moba_slope_split()
    return pl.pallas_call(
        functools.partial(_proj_kernel, slope_hi=slope_hi, slope_lo=slope_lo),
        grid=(b, s // PROJ_ROWS),
        in_specs=[
            pl.BlockSpec((1, PROJ_ROWS, d), lambda i, t: (i, t, 0)),
            pl.BlockSpec((1, d), lambda i, t: (0, 0)),
            pl.BlockSpec((d, 3 * d), lambda i, t: (0, 0)),
        ],
        out_specs=[
            pl.BlockSpec((1, PROJ_ROWS, 3 * WIDTH_A), lambda i, t: (i, t, 0)),
            pl.BlockSpec((1, LANES, PROJ_ROWS), lambda i, t: (i, 0, t)),
            pl.BlockSpec((1, 4, PROJ_ROWS // 4, 3 * WIDTH_A), lambda i, t: (i, 0, t, 0)),
            pl.BlockSpec((1, 16, PROJ_ROWS // 16, 3 * WIDTH_A), lambda i, t: (i, 0, t, 0)),
            pl.BlockSpec((1, PROJ_ROWS, 6 * WIDTH_B), lambda i, t: (i, t, 0)),
        ],
        out_shape=[
            jax.ShapeDtypeStruct((b, s, 3 * WIDTH_A), jnp.bfloat16),
            jax.ShapeDtypeStruct((b, LANES, s), jnp.float32),
            jax.ShapeDtypeStruct((b, 4, s // 4, 3 * WIDTH_A), jnp.bfloat16),
            jax.ShapeDtypeStruct((b, 16, s // 16, 3 * WIDTH_A), jnp.bfloat16),
            jax.ShapeDtypeStruct((b, s, 6 * WIDTH_B), jnp.bfloat16),
        ],
        scratch_shapes=[pltpu.VMEM((n_blocks, WIDTH_B), jnp.float32),
                        pltpu.VMEM((4, PROJ_ROWS, LANES), jnp.float32)],
        compiler_params=pltpu.CompilerParams(
            dimension_semantics=("arbitrary", "arbitrary"), vmem_limit_bytes=VMEM_LIMIT),
        name="proj",
    )(x, g_attn, w_in_bf16)


def _select_kernel(gate_ref, sel_ref):
    n_blocks = LANES // N_HEADS_B
    width = gate_ref.shape[2]
    q_pos = pl.program_id(1) * width + lax.broadcasted_iota(jnp.int32, (n_blocks, width), 1)
    blk = lax.broadcasted_iota(jnp.int32, (n_blocks, width), 0)
    past = blk < q_pos // MOBA_BLOCK
    for h in range(N_HEADS_B):
        g = jnp.where(past, gate_ref[0, h * n_blocks:(h + 1) * n_blocks, :], NEG_INF)
        rank = jnp.zeros(g.shape, jnp.int32)
        for other_blk in range(n_blocks):
            other = g[other_blk:other_blk + 1, :]
            beats = (other > g) | ((other == g) & (other_blk < blk))
            rank = rank + beats.astype(jnp.int32)
        sel_ref[0, h * n_blocks:(h + 1) * n_blocks, :] = (past & (rank < MOBA_TOPK)).astype(sel_ref.dtype)


def _select(gate):
    b, _, s = gate.shape
    width = 1024
    return pl.pallas_call(
        _select_kernel,
        grid=(b, s // width),
        in_specs=[pl.BlockSpec((1, LANES, width), lambda i, t: (i, 0, t))],
        out_specs=pl.BlockSpec((1, LANES, width), lambda i, t: (i, 0, t)),
        out_shape=jax.ShapeDtypeStruct((b, LANES, s), jnp.float32),
        compiler_params=pltpu.CompilerParams(dimension_semantics=("arbitrary", "arbitrary")),
        name="select",
    )(gate)


def _dilated_kernel(slopes_ref, q1_ref, k1_ref, v1_ref, q4_ref, k4_ref, v4_ref,
                    q16_ref, k16_ref, v16_ref, o_ref,
                    acc_ref, stat_ref, bias_ref, s_ref, p_ref, mloc_ref, stats_ref):
    pair = pl.program_id(0)
    seq = k1_ref.shape[1]
    c = DIL_BLOCK
    n_steps = seq // c
    dilations = tuple(d for _, d in DILATED_PAIRS)

    @pl.when(pl.program_id(1) == 0)
    def _():
        key = lax.broadcasted_iota(jnp.int32, (2 * c, 2 * c), 0)
        lane = lax.broadcasted_iota(jnp.int32, (2 * c, 2 * c), 1)
        slope_l = jnp.where(lane >= c, slopes_ref[2 * pair + 1], slopes_ref[2 * pair])
        for b_idx, dil in enumerate(dilations):
            for first in range(2):
                dist = (0 if first else c) + jnp.bitwise_and(lane, c - 1) - key
                bias_ref[2 * b_idx + first] = jnp.where(
                    (dist >= 0) & (dist <= c), (-dil) * slope_l * dist.astype(jnp.float32), NEG_INF)

    head0 = lax.broadcasted_iota(jnp.int32, (c, LANES), 1) < HEAD_DIM

    branches = ((q1_ref, k1_ref, v1_ref), (q4_ref, k4_ref, v4_ref), (q16_ref, k16_ref, v16_ref))
    for b_idx, (dil, (q_ref, k_ref, v_ref)) in enumerate(zip(dilations, branches)):
        sub_len = seq // dil
        n_chunks = sub_len // c

        def offsets(t, n_chunks=n_chunks, sub_len=sub_len):
            t = jnp.minimum(t, n_steps - 1)
            r = t // n_chunks
            n = t % n_chunks
            q_off = pl.multiple_of(r * sub_len + n * c, c)
            k_off = pl.multiple_of(r * sub_len + jnp.maximum(n - 1, 0) * c, c)
            return r, n, q_off, k_off

        def scores(t, slot, b_idx=b_idx, q_ref=q_ref, k_ref=k_ref, offsets=offsets):
            _, n, q_off, k_off = offsets(t)
            q = q_ref[0, pl.ds(q_off, c), :]
            q_heads = jnp.concatenate([jnp.where(head0, q, jnp.zeros_like(q)),
                                       jnp.where(head0, jnp.zeros_like(q), q)], axis=0)
            keys = k_ref[0, pl.ds(k_off, 2 * c), :]
            first = (n == 0).astype(jnp.int32)
            st = _dot_nt(keys, q_heads) + bias_ref[2 * b_idx + first]
            s_ref[slot] = st
            mloc_ref[slot, 0:1, :] = jnp.max(st, axis=0, keepdims=True)

        def softmax(t, slot):
            m_loc = mloc_ref[slot, 0:1, :]
            p = jnp.exp2(s_ref[slot] - m_loc)
            stats_ref[slot, 0:1, :] = m_loc
            stats_ref[slot, 1:2, :] = jnp.sum(p, axis=0, keepdims=True)
            p_ref[slot] = p.astype(p_ref.dtype)

        def emit(t, slot, b_idx=b_idx, dil=dil, v_ref=v_ref, offsets=offsets):
            r, n, _, k_off = offsets(t)
            vals = v_ref[0, pl.ds(k_off, 2 * c), :]
            pv = lax.dot_general(vals, p_ref[slot], (((0,), (0,)), ((), ())),
                                 preferred_element_type=jnp.float32)
            out_t = jnp.concatenate([pv[:HEAD_DIM, :c], pv[HEAD_DIM:, c:]], axis=0)
            if dil == 1:
                rows = pl.ds(pl.multiple_of(n * c, c), c)
            else:
                rows = pl.ds(r + dil * c * n, c, stride=dil)
            l_loc = stats_ref[slot, 1:2, :]
            inv_l = 1.0 / l_loc
            lse = stats_ref[slot, 0:1, :] + jnp.log2(l_loc)
            out_t = out_t * jnp.concatenate([jnp.broadcast_to(inv_l[:, :c], (HEAD_DIM, c)),
                                             jnp.broadcast_to(inv_l[:, c:], (HEAD_DIM, c))], axis=0)
            lse_t = jnp.concatenate([jnp.broadcast_to(lse[:, :c], (HEAD_DIM, c)),
                                     jnp.broadcast_to(lse[:, c:], (HEAD_DIM, c))], axis=0)
            acc_ref[b_idx, rows, :] = out_t.T
            stat_ref[b_idx, rows, :] = lse_t.T

        _software_pipeline(n_steps, scores, softmax, emit, DIL_LAG)

    def merge(ci, carry):
        rows = pl.ds(pl.multiple_of(ci * c, c), c)
        lses = [stat_ref[b_idx, rows, :] for b_idx in range(len(dilations))]
        lse_max = functools.reduce(jnp.maximum, lses)
        weights = [jnp.exp2(lse - lse_max) for lse in lses]
        out = functools.reduce(lambda a, b: a + b,
                               [w * acc_ref[b_idx, rows, :] for b_idx, w in enumerate(weights)])
        o_ref[0, rows, :] = (out / functools.reduce(lambda a, b: a + b, weights)).astype(o_ref.dtype)
        return carry

    lax.fori_loop(0, seq // c, merge, 0)


def _dilated(qkv_by_dilation, slopes):
    b, s, _ = qkv_by_dilation[0].shape
    n_pairs = WIDTH_A // LANES
    tok_major = pl.BlockSpec((1, s, LANES), lambda p, i: (i, 0, p))
    operands, in_specs = [slopes], [pl.BlockSpec(memory_space=pltpu.SMEM)]
    for arr in qkv_by_dilation:
        for part in range(3):
            operands.append(arr)
            in_specs.append(pl.BlockSpec((1, s, LANES), lambda p, i, part=part: (i, 0, part * n_pairs + p)))
    n_br = len(DILATED_PAIRS)
    c = DIL_BLOCK
    return pl.pallas_call(
        _dilated_kernel,
        grid=(n_pairs, b),
        in_specs=in_specs,
        out_specs=tok_major,
        out_shape=jax.ShapeDtypeStruct((b, s, WIDTH_A), jnp.float32),
        scratch_shapes=[pltpu.VMEM((n_br, s, LANES), jnp.float32),
                        pltpu.VMEM((n_br, s, LANES), jnp.float32),
                        pltpu.VMEM((2 * n_br, 2 * c, 2 * c), jnp.float32),
                        pltpu.VMEM((2 * DIL_LAG, 2 * c, 2 * c), jnp.float32),
                        pltpu.VMEM((2 * DIL_LAG, 2 * c, 2 * c), jnp.bfloat16),
                        pltpu.VMEM((2 * DIL_LAG, 8, 2 * c), jnp.float32),
                        pltpu.VMEM((2 * DIL_LAG, 8, 2 * c), jnp.float32)],
        compiler_params=pltpu.CompilerParams(
            dimension_semantics=("arbitrary", "arbitrary"), vmem_limit_bytes=VMEM_LIMIT),
        name="dilated",
    )(*operands)


M_INIT = -1e30


def _moba_pair_tables(n_blocks):
    pairs = [(i, j) for i in range(1, n_blocks) for j in range(i)]
    pairs += [pairs[-1]] * MOBA_RING
    return np.array([p[0] for p in pairs], np.int32), np.array([p[1] for p in pairs], np.int32)


def _moba_kernel(slopes_ref, qblk_ref, kblk_ref, q0_ref, q1_ref, k0_ref, k1_ref, v0_ref, v1_ref, sel_ref,
                 o_ref, acc_ref, m_ref, l_ref, mask_ref, *rings):
    s_ref, p_ref, mloc_ref, stats_ref = (rings[n * MOBA_RING:(n + 1) * MOBA_RING] for n in range(4))
    pair = pl.program_id(0)
    bs = MOBA_BLOCK
    n_blocks = k0_ref.shape[1] // bs
    n_past = qblk_ref.shape[0] - MOBA_RING
    sel_rows = LANES // N_HEADS_B
    slopes = (slopes_ref[2 * pair], slopes_ref[2 * pair + 1])
    q_refs, k_refs, v_refs = (q0_ref, q1_ref), (k0_ref, k1_ref), (v0_ref, v1_ref)
    sum_row = (HEAD_DIM, 0)

    @pl.when(pl.program_id(1) == 0)
    def _():
        rel = (lax.broadcasted_iota(jnp.int32, (bs, bs), 0)
               - lax.broadcasted_iota(jnp.int32, (bs, bs), 1))
        mask_ref[...] = jnp.where(rel <= 0, 0.0, NEG_INF)

    m_ref[...] = jnp.full_like(m_ref, M_INIT)
    l_ref[...] = jnp.zeros_like(l_ref)
    acc_ref[...] = jnp.zeros_like(acc_ref)
    q_pos = lax.broadcasted_iota(jnp.int32, (1, bs), 1).astype(jnp.float32)

    def make_stages(blocks_of, own):
        def scores(t, slot):
            i, j = blocks_of(t)
            q_off = pl.multiple_of(i * bs, bs)
            k_off = pl.multiple_of(j * bs, bs)
            for h in range(2):
                st = _dot_nt(k_refs[h][0, pl.ds(k_off, bs), :], q_refs[h][0, pl.ds(q_off, bs), :])
                if own:
                    st = st + mask_ref[...]
                s_ref[slot][h] = st
                mloc_ref[slot][h:h + 1, :] = jnp.max(st, axis=0, keepdims=True)

        def softmax(t, slot):
            for h in range(2):
                m_loc = mloc_ref[slot][h:h + 1, :]
                stats_ref[slot][h:h + 1, :] = m_loc
                p_ref[slot][h] = jnp.exp2(s_ref[slot][h] - m_loc).astype(p_ref[slot].dtype)

        def merge(t, slot):
            i, j = blocks_of(t)
            q_off = pl.multiple_of(i * bs, bs)
            k_off = pl.multiple_of(j * bs, bs)
            gap = ((i - j) * bs).astype(jnp.float32)
            for h in range(2):
                pv = lax.dot_general(v_refs[h][0, pl.ds(k_off, bs), :], p_ref[slot][h],
                                     (((0,), (0,)), ((), ())), preferred_element_type=jnp.float32)
                out_t = pv[h * HEAD_DIM:(h + 1) * HEAD_DIM]
                l_blk = pv[sum_row[h]:sum_row[h] + 1]
                m_blk = stats_ref[slot][h:h + 1, :] - slopes[h] * (q_pos + gap)
                m_old = m_ref[h:h + 1, pl.ds(q_off, bs)]
                if own:
                    m_new = jnp.maximum(m_old, m_blk)
                    a_blk = jnp.exp2(m_blk - m_new)
                else:
                    chosen = sel_ref[0, pl.ds(h * sel_rows + j, 1), pl.ds(q_off, bs)] > 0.5
                    m_new = jnp.where(chosen, jnp.maximum(m_old, m_blk), m_old)
                    a_blk = jnp.where(chosen, jnp.exp2(m_blk - m_new), 0.0)
                a_old = jnp.exp2(m_old - m_new)
                acc_ref[h, :, pl.ds(q_off, bs)] = a_old * acc_ref[h, :, pl.ds(q_off, bs)] + a_blk * out_t
                l_ref[h:h + 1, pl.ds(q_off, bs)] = a_old * l_ref[h:h + 1, pl.ds(q_off, bs)] + a_blk * l_blk
                m_ref[h:h + 1, pl.ds(q_off, bs)] = m_new

        return scores, softmax, merge

    def own_blocks(t):
        i = jnp.minimum(t, n_blocks - 1)
        return i, i

    _software_pipeline(n_blocks, *make_stages(own_blocks, own=True), MOBA_LAG, scores_first=False)
    _software_pipeline(n_past, *make_stages(lambda t: (qblk_ref[t], kblk_ref[t]), own=False),
                       MOBA_LAG, scores_first=False)

    def finish(i, carry):
        q0 = pl.multiple_of(i * bs, bs)
        halves = [acc_ref[h, :, pl.ds(q0, bs)] / l_ref[h:h + 1, pl.ds(q0, bs)] for h in range(2)]
        o_ref[0, pl.ds(q0, bs), :] = jnp.concatenate(halves, axis=0).T.astype(o_ref.dtype)
        return carry

    lax.fori_loop(0, n_blocks, finish, 0)


def _moba(heads, sel, slopes):
    b, s, _ = heads.shape
    n_pairs = WIDTH_B // LANES
    sel_rows = 2 * (LANES // N_HEADS_B)
    qblk, kblk = _moba_pair_tables(s // MOBA_BLOCK)
    smem = pl.BlockSpec(memory_space=pltpu.SMEM)
    tile = lambda n: pl.BlockSpec((1, s, LANES), lambda p, i: (i, 0, 6 * p + n))
    return pl.pallas_call(
        _moba_kernel,
        grid=(n_pairs, b),
        in_specs=[smem, smem, smem] + [tile(n) for n in range(6)]
                 + [pl.BlockSpec((1, sel_rows, s), lambda p, i: (i, p, 0))],
        out_specs=pl.BlockSpec((1, s, LANES), lambda p, i: (i, 0, p)),
        out_shape=jax.ShapeDtypeStruct((b, s, WIDTH_B), jnp.float32),
        scratch_shapes=[pltpu.VMEM((2, HEAD_DIM, s), jnp.float32),
                        pltpu.VMEM((8, s), jnp.float32),
                        pltpu.VMEM((8, s), jnp.float32),
                        pltpu.VMEM((MOBA_BLOCK, MOBA_BLOCK), jnp.float32)]
                       + [pltpu.VMEM((2, MOBA_BLOCK, MOBA_BLOCK), jnp.float32)] * MOBA_RING
                       + [pltpu.VMEM((2, MOBA_BLOCK, MOBA_BLOCK), jnp.bfloat16)] * MOBA_RING
                       + [pltpu.VMEM((8, MOBA_BLOCK), jnp.float32)] * (2 * MOBA_RING),
        compiler_params=pltpu.CompilerParams(
            dimension_semantics=("arbitrary", "arbitrary"), vmem_limit_bytes=VMEM_LIMIT),
        name="moba",
    )(slopes, jnp.asarray(qblk), jnp.asarray(kblk), *([heads] * 6), sel)


def _rms(x, g):
    return (x * lax.rsqrt(jnp.mean(x * x, axis=-1, keepdims=True) + EPS)) * g


def _tail_kernel(x_ref, p_ref, oa_ref, ob_ref, ga_ref, gb_ref, wout_ref, gmlp_ref, wup_ref, wdown_ref,
                 gple_ref, wgate_ref, bgate_ref, wproj_ref, gfin_ref, o_ref):
    bf = jnp.bfloat16
    ya = _rms(oa_ref[0].astype(jnp.float32), ga_ref[...]).astype(bf)
    yb = _rms(ob_ref[0].astype(jnp.float32), gb_ref[...]).astype(bf)
    y = (jnp.dot(ya, wout_ref[0:WIDTH_A, :], preferred_element_type=jnp.float32)
         + jnp.dot(yb, wout_ref[WIDTH_A:, :], preferred_element_type=jnp.float32))
    h = x_ref[0] + y

    hn = _rms(h, gmlp_ref[...]).astype(bf)
    ff_chunk = 1024
    for c in range(D_FF // ff_chunk):
        u = jnp.dot(hn, wup_ref[:, c * ff_chunk:(c + 1) * ff_chunk], preferred_element_type=jnp.float32)
        u = jnp.square(jnp.maximum(u, 0.0)).astype(bf)
        h = h + jnp.dot(u, wdown_ref[c * ff_chunk:(c + 1) * ff_chunk, :], preferred_element_type=jnp.float32)

    z = jnp.dot(_rms(h, gple_ref[...]).astype(bf), wgate_ref[...], preferred_element_type=jnp.float32)
    gate = jax.nn.sigmoid(z + bgate_ref[...])
    h = h + gate * jnp.dot(p_ref[0].astype(bf), wproj_ref[...], preferred_element_type=jnp.float32)
    o_ref[0] = _rms(h, gfin_ref[...]).astype(o_ref.dtype)


def _tail(x, p, oa, ob, ga, gb, wout, gmlp, wup, wdown, gple, wgate, bgate, wproj, gfin):
    b, s, d = x.shape
    rows = lambda width: pl.BlockSpec((1, TAIL_ROWS, width), lambda i, t: (i, t, 0))
    full = lambda arr: pl.BlockSpec(arr.shape, lambda i, t: (0,) * arr.ndim,
                                    pipeline_mode=pl.Buffered(1))
    consts = (ga, gb, wout, gmlp, wup, wdown, gple, wgate, bgate, wproj, gfin)
    return pl.pallas_call(
        _tail_kernel,
        grid=(b, s // TAIL_ROWS),
        in_specs=[rows(d), rows(PLE_DIM), rows(WIDTH_A), rows(WIDTH_B)] + [full(a) for a in consts],
        out_specs=rows(d),
        out_shape=jax.ShapeDtypeStruct((b, s, d), x.dtype),
        compiler_params=pltpu.CompilerParams(
            dimension_semantics=("arbitrary", "arbitrary"), vmem_limit_bytes=VMEM_LIMIT),
        name="tail",
    )(x, p, oa, ob, *consts)


def kernel(x, p, g_attn, w_in, g_out_a, g_out_b, w_out, g_mlp, w_up, w_down, g_ple, w_ple_gate,
           b_ple_gate, w_ple_proj, g_final):
    assert x.shape[1:] == (4096, D_MODEL) and p.shape[0] == 1 and w_in.shape[0] == 1
    b, s, _ = x.shape
    bf = jnp.bfloat16
    slopes_a, slopes_b = _alibi_log2_slopes()
    row = lambda v: v.reshape(1, -1)

    qkv_a, gate, qkv_a4, qkv_a16, moba_heads = _project(x, row(g_attn[0]), w_in[0].astype(bf))
    sel = _select(gate)
    oa = _dilated([qkv_a, qkv_a4.reshape(b, s, 3 * WIDTH_A), qkv_a16.reshape(b, s, 3 * WIDTH_A)],
                  jnp.asarray(slopes_a))
    ob = _moba(moba_heads, sel, jnp.asarray(slopes_b))

    return _tail(x, p[0], oa, ob, row(g_out_a[0]), row(g_out_b[0]), w_out[0].astype(bf), row(g_mlp[0]),
                 w_up[0].astype(bf), w_down[0].astype(bf), row(g_ple[0]), w_ple_gate[0].astype(bf),
                 row(b_ple_gate[0]), w_ple_proj[0].astype(bf), row(g_final))
```

```python
import functools
import math

import jax
import jax.numpy as jnp
import numpy as np
from jax import lax
from jax.experimental import pallas as pl
from jax.experimental.pallas import tpu as pltpu

D_MODEL = 1024
HEAD_DIM = 64
N_HEADS = D_MODEL // HEAD_DIM
N_HEADS_A = N_HEADS // 2
N_HEADS_B = N_HEADS - N_HEADS_A
WIDTH_A = N_HEADS_A * HEAD_DIM
WIDTH_B = N_HEADS_B * HEAD_DIM
DILATED_PAIRS = ((128, 1), (512, 4), (2048, 16))
MOBA_BLOCK = 256
MOBA_TOPK = 3
D_FF = 4 * D_MODEL
PLE_DIM = 256
EPS = 1e-6

LANES = 128
LOG2E = 1.4426950408889634
Q_SCALE = HEAD_DIM ** -0.5 * LOG2E
NEG_INF = float("-inf")
VMEM_LIMIT = 56 * 1024 * 1024

PROJ_ROWS = 512
TAIL_ROWS = 512
DIL_BLOCK = 128


def _alibi_log2_slopes():
    s = np.array([2.0 ** (-8.0 * (i + 1) / N_HEADS) for i in range(N_HEADS)], dtype=np.float64)
    return (s[0::2] * LOG2E).astype(np.float32), (s[1::2] * LOG2E).astype(np.float32)


def _dot_nt(a, b):
    return lax.dot_general(a, b, (((1,), (1,)), ((), ())), preferred_element_type=jnp.float32)


def _split_bf16(x):
    hi = x.astype(jnp.bfloat16)
    lo = (x - hi.astype(jnp.float32)).astype(jnp.bfloat16)
    return hi, lo


DIL_LAG = 4
MOBA_OWN_LAG = 4
MOBA_LAG = 6
MOBA_RING = 2 * MOBA_LAG


def _software_pipeline(n_steps, scores, softmax, emit, lag, scores_first=True):
    ring = 2 * lag
    assert n_steps % ring == 0
    for t in range(ring):
        scores(t, t)
    for t in range(lag):
        softmax(t, t)

    def body(it, carry):
        for k in range(ring):
            t = ring * it + k
            if scores_first:
                scores(t + ring, k)
            softmax(t + lag, (k + lag) % ring)
            emit(t, k)
            if not scores_first:
                scores(t + ring, k)
        return carry

    lax.fori_loop(0, n_steps // ring, body, 0)


def _moba_slope_split():
    _, slopes = _alibi_log2_slopes()
    hi = slopes.astype(jnp.bfloat16).astype(np.float32)
    lo = (slopes - hi).astype(jnp.bfloat16).astype(np.float32)
    return hi, lo


def _proj_kernel(x_ref, g_ref, w_ref, qkv_ref, gate_ref, a4_ref, a16_ref, moba_ref, kmean_ref, slab_ref,
                 *, slope_hi, slope_lo):
    t = pl.program_id(1)
    x = x_ref[0]
    ms = jnp.mean(x * x, axis=-1, keepdims=True)
    hn = (x * lax.rsqrt(ms + EPS)) * g_ref[...]
    qkv = jnp.dot(hn.astype(jnp.bfloat16), w_ref[...], preferred_element_type=jnp.float32)

    qa = qkv[:, 0:WIDTH_A]
    qb = qkv[:, 3 * WIDTH_A:3 * WIDTH_A + WIDTH_B]
    kb = qkv[:, 3 * WIDTH_A + WIDTH_B:3 * WIDTH_A + 2 * WIDTH_B]
    vb = qkv[:, 3 * WIDTH_A + 2 * WIDTH_B:]
    qkv_ref[0, :, 0:WIDTH_A] = (qa * Q_SCALE).astype(qkv_ref.dtype)
    qkv_ref[0, :, WIDTH_A:] = qkv[:, WIDTH_A:3 * WIDTH_A].astype(qkv_ref.dtype)

    lane = lax.broadcasted_iota(jnp.int32, (PROJ_ROWS, LANES), 1)
    key_pos = jnp.bitwise_and(lax.broadcasted_iota(jnp.int32, (PROJ_ROWS, LANES), 0),
                              MOBA_BLOCK - 1).astype(jnp.float32)
    for pair in range(WIDTH_B // LANES):
        cols = slice(pair * LANES, (pair + 1) * LANES)
        q_t, k_t, v_t = qb[:, cols] * Q_SCALE, kb[:, cols], vb[:, cols]
        for h in range(2):
            own = (lane < HEAD_DIM) if h == 0 else (lane >= HEAD_DIM)
            f0 = HEAD_DIM if h == 0 else 0
            head = 2 * pair + h
            q_feat = jnp.where(lane == f0, float(slope_hi[head]),
                               jnp.where(lane == f0 + 1, float(slope_lo[head]), 0.0))
            k_feat = jnp.where((lane == f0) | (lane == f0 + 1), key_pos, 0.0)
            base = (pair * 6 + h) * LANES
            for n, tile in enumerate((jnp.where(own, q_t, q_feat), jnp.where(own, k_t, k_feat),
                                      jnp.where(own, v_t, 1.0))):
                moba_ref[0, :, base + 2 * n * LANES:base + (2 * n + 1) * LANES] = tile.astype(moba_ref.dtype)

    rows4 = PROJ_ROWS // 4
    rows16 = PROJ_ROWS // 16
    for cb in range(3 * WIDTH_A // LANES):
        cols = slice(cb * LANES, (cb + 1) * LANES)
        val = qkv[:, cols]
        nat, res4 = 2 * (cb % 2), 2 * (cb % 2) + 1
        slab_ref[nat] = val * Q_SCALE if cb < WIDTH_A // LANES else val
        for r4 in range(4):
            part = slab_ref[nat, pl.ds(r4, rows4, stride=4), :]
            a4_ref[0, r4, :, cols] = part.astype(a4_ref.dtype)
            slab_ref[res4, r4 * rows4:(r4 + 1) * rows4, :] = part
        for r4 in range(4):
            for j in range(4):
                part = slab_ref[res4, pl.ds(r4 * rows4 + j, rows16, stride=4), :]
                a16_ref[0, 4 * j + r4, :, cols] = part.astype(a16_ref.dtype)

    @pl.when(t == 0)
    def _():
        kmean_ref[...] = jnp.zeros_like(kmean_ref)

    blocks_per_tile = PROJ_ROWS // MOBA_BLOCK
    for sb in range(blocks_per_tile):
        km = jnp.mean(kb[sb * MOBA_BLOCK:(sb + 1) * MOBA_BLOCK, :], axis=0, keepdims=True)
        kmean_ref[pl.ds(t * blocks_per_tile + sb, 1), :] = km

    kmean = kmean_ref[...]
    lane_head = lax.broadcasted_iota(jnp.int32, kmean.shape, 1) // HEAD_DIM
    wt = jnp.concatenate(
        [jnp.where(lane_head == h, kmean, 0.0) for h in range(N_HEADS_B)], axis=0)
    q_hi, q_lo = _split_bf16(qb)
    w_hi, w_lo = _split_bf16(wt)
    gate_ref[0] = _dot_nt(w_hi, q_hi) + (_dot_nt(w_lo, q_hi) + _dot_nt(w_hi, q_lo))


def _project(x, g_attn, w_in_bf16):
    b, s, d = x.shape
    n_blocks = s // MOBA_BLOCK
    slope_hi, slope_lo = _moba_slope_split()
    return pl.pallas_call(
        functools.partial(_proj_kernel, slope_hi=slope_hi, slope_lo=slope_lo),
        grid=(b, s // PROJ_ROWS),
        in_specs=[
            pl.BlockSpec((1, PROJ_ROWS, d), lambda i, t: (i, t, 0)),
            pl.BlockSpec((1, d), lambda i, t: (0, 0)),
            pl.BlockSpec((d, 3 * d), lambda i, t: (0, 0)),
        ],
        out_specs=[
            pl.BlockSpec((1, PROJ_ROWS, 3 * WIDTH_A), lambda i, t: (i, t, 0)),
            pl.BlockSpec((1, LANES, PROJ_ROWS), lambda i, t: (i, 0, t)),
            pl.BlockSpec((1, 4, PROJ_ROWS // 4, 3 * WIDTH_A), lambda i, t: (i, 0, t, 0)),
            pl.BlockSpec((1, 16, PROJ_ROWS // 16, 3 * WIDTH_A), lambda i, t: (i, 0, t, 0)),
            pl.BlockSpec((1, PROJ_ROWS, 6 * WIDTH_B), lambda i, t: (i, t, 0)),
        ],
        out_shape=[
            jax.ShapeDtypeStruct((b, s, 3 * WIDTH_A), jnp.bfloat16),
            jax.ShapeDtypeStruct((b, LANES, s), jnp.float32),
            jax.ShapeDtypeStruct((b, 4, s // 4, 3 * WIDTH_A), jnp.bfloat16),
            jax.ShapeDtypeStruct((b, 16, s // 16, 3 * WIDTH_A), jnp.bfloat16),
            jax.ShapeDtypeStruct((b, s, 6 * WIDTH_B), jnp.bfloat16),
        ],
        scratch_shapes=[pltpu.VMEM((n_blocks, WIDTH_B), jnp.float32),
                        pltpu.VMEM((4, PROJ_ROWS, LANES), jnp.float32)],
        compiler_params=pltpu.CompilerParams(
            dimension_semantics=("arbitrary", "arbitrary"), vmem_limit_bytes=VMEM_LIMIT),
        name="proj",
    )(x, g_attn, w_in_bf16)


def _select_kernel(gate_ref, sel_ref):
    n_blocks = LANES // N_HEADS_B
    width = gate_ref.shape[2]
    q_pos = pl.program_id(1) * width + lax.broadcasted_iota(jnp.int32, (n_blocks, width), 1)
    blk = lax.broadcasted_iota(jnp.int32, (n_blocks, width), 0)
    past = blk < q_pos // MOBA_BLOCK
    for h in range(N_HEADS_B):
        g = jnp.where(past, gate_ref[0, h * n_blocks:(h + 1) * n_blocks, :], NEG_INF)
        rank = jnp.zeros(g.shape, jnp.int32)
        for other_blk in range(n_blocks):
            other = g[other_blk:other_blk + 1, :]
            beats = (other > g) | ((other == g) & (other_blk < blk))
            rank = rank + beats.astype(jnp.int32)
        sel_ref[0, h * n_blocks:(h + 1) * n_blocks, :] = (past & (rank < MOBA_TOPK)).astype(sel_ref.dtype)


def _select(gate):
    b, _, s = gate.shape
    width = 1024
    return pl.pallas_call(
        _select_kernel,
        grid=(b, s // width),
        in_specs=[pl.BlockSpec((1, LANES, width), lambda i, t: (i, 0, t))],
        out_specs=pl.BlockSpec((1, LANES, width), lambda i, t: (i, 0, t)),
        out_shape=jax.ShapeDtypeStruct((b, LANES, s), jnp.float32),
        compiler_params=pltpu.CompilerParams(dimension_semantics=("arbitrary", "arbitrary")),
        name="select",
    )(gate)


def _dilated_kernel(slopes_ref, q1_ref, k1_ref, v1_ref, q4_ref, k4_ref, v4_ref,
                    q16_ref, k16_ref, v16_ref, o_ref,
                    acc_ref, stat_ref, bias_ref, s_ref, p_ref, mloc_ref, stats_ref):
    pair = pl.program_id(0)
    seq = k1_ref.shape[1]
    c = DIL_BLOCK
    n_steps = seq // c
    dilations = tuple(d for _, d in DILATED_PAIRS)

    @pl.when(pl.program_id(1) == 0)
    def _():
        key = lax.broadcasted_iota(jnp.int32, (2 * c, 2 * c), 0)
        lane = lax.broadcasted_iota(jnp.int32, (2 * c, 2 * c), 1)
        slope_l = jnp.where(lane >= c, slopes_ref[2 * pair + 1], slopes_ref[2 * pair])
        for b_idx, dil in enumerate(dilations):
            for first in range(2):
                dist = (0 if first else c) + jnp.bitwise_and(lane, c - 1) - key
                bias_ref[2 * b_idx + first] = jnp.where(
                    (dist >= 0) & (dist <= c), (-dil) * slope_l * dist.astype(jnp.float32), NEG_INF)

    head0 = lax.broadcasted_iota(jnp.int32, (c, LANES), 1) < HEAD_DIM

    branches = ((q1_ref, k1_ref, v1_ref), (q4_ref, k4_ref, v4_ref), (q16_ref, k16_ref, v16_ref))
    for b_idx, (dil, (q_ref, k_ref, v_ref)) in enumerate(zip(dilations, branches)):
        sub_len = seq // dil
        n_chunks = sub_len // c

        def offsets(t, n_chunks=n_chunks, sub_len=sub_len):
            t = jnp.minimum(t, n_steps - 1)
            r = t // n_chunks
            n = t % n_chunks
            q_off = pl.multiple_of(r * sub_len + n * c, c)
            k_off = pl.multiple_of(r * sub_len + jnp.maximum(n - 1, 0) * c, c)
            return r, n, q_off, k_off

        def scores(t, slot, b_idx=b_idx, q_ref=q_ref, k_ref=k_ref, offsets=offsets):
            _, n, q_off, k_off = offsets(t)
            q = q_ref[0, pl.ds(q_off, c), :]
            q_heads = jnp.concatenate([jnp.where(head0, q, jnp.zeros_like(q)),
                                       jnp.where(head0, jnp.zeros_like(q), q)], axis=0)
            keys = k_ref[0, pl.ds(k_off, 2 * c), :]
            first = (n == 0).astype(jnp.int32)
            st = _dot_nt(keys, q_heads) + bias_ref[2 * b_idx + first]
            s_ref[slot] = st
            mloc_ref[slot, 0:1, :] = jnp.max(st, axis=0, keepdims=True)

        def softmax(t, slot):
            m_loc = mloc_ref[slot, 0:1, :]
            p = jnp.exp2(s_ref[slot] - m_loc)
            stats_ref[slot, 0:1, :] = m_loc
            stats_ref[slot, 1:2, :] = jnp.sum(p, axis=0, keepdims=True)
            p_ref[slot] = p.astype(p_ref.dtype)

        def emit(t, slot, b_idx=b_idx, dil=dil, v_ref=v_ref, offsets=offsets):
            r, n, _, k_off = offsets(t)
            vals = v_ref[0, pl.ds(k_off, 2 * c), :]
            pv = lax.dot_general(vals, p_ref[slot], (((0,), (0,)), ((), ())),
                                 preferred_element_type=jnp.float32)
            out_t = jnp.concatenate([pv[:HEAD_DIM, :c], pv[HEAD_DIM:, c:]], axis=0)
            if dil == 1:
                rows = pl.ds(pl.multiple_of(n * c, c), c)
            else:
                rows = pl.ds(r + dil * c * n, c, stride=dil)
            l_loc = stats_ref[slot, 1:2, :]
            inv_l = 1.0 / l_loc
            lse = stats_ref[slot, 0:1, :] + jnp.log2(l_loc)
            out_t = out_t * jnp.concatenate([jnp.broadcast_to(inv_l[:, :c], (HEAD_DIM, c)),
                                             jnp.broadcast_to(inv_l[:, c:], (HEAD_DIM, c))], axis=0)
            lse_t = jnp.concatenate([jnp.broadcast_to(lse[:, :c], (HEAD_DIM, c)),
                                     jnp.broadcast_to(lse[:, c:], (HEAD_DIM, c))], axis=0)
            acc_ref[b_idx, rows, :] = out_t.T
            stat_ref[b_idx, rows, :] = lse_t.T

        _software_pipeline(n_steps, scores, softmax, emit, DIL_LAG)

    def merge(ci, carry):
        rows = pl.ds(pl.multiple_of(ci * c, c), c)
        lses = [stat_ref[b_idx, rows, :] for b_idx in range(len(dilations))]
        lse_max = functools.reduce(jnp.maximum, lses)
        weights = [jnp.exp2(lse - lse_max) for lse in lses]
        out = functools.reduce(lambda a, b: a + b,
                               [w * acc_ref[b_idx, rows, :] for b_idx, w in enumerate(weights)])
        o_ref[0, rows, :] = (out / functools.reduce(lambda a, b: a + b, weights)).astype(o_ref.dtype)
        return carry

    lax.fori_loop(0, seq // c, merge, 0)


def _dilated(qkv_by_dilation, slopes):
    b, s, _ = qkv_by_dilation[0].shape
    n_pairs = WIDTH_A // LANES
    tok_major = pl.BlockSpec((1, s, LANES), lambda p, i: (i, 0, p))
    operands, in_specs = [slopes], [pl.BlockSpec(memory_space=pltpu.SMEM)]
    for arr in qkv_by_dilation:
        for part in range(3):
            operands.append(arr)
            in_specs.append(pl.BlockSpec((1, s, LANES), lambda p, i, part=part: (i, 0, part * n_pairs + p)))
    n_br = len(DILATED_PAIRS)
    c = DIL_BLOCK
    return pl.pallas_call(
        _dilated_kernel,
        grid=(n_pairs, b),
        in_specs=in_specs,
        out_specs=tok_major,
        out_shape=jax.ShapeDtypeStruct((b, s, WIDTH_A), jnp.float32),
        scratch_shapes=[pltpu.VMEM((n_br, s, LANES), jnp.float32),
                        pltpu.VMEM((n_br, s, LANES), jnp.float32),
                        pltpu.VMEM((2 * n_br, 2 * c, 2 * c), jnp.float32),
                        pltpu.VMEM((2 * DIL_LAG, 2 * c, 2 * c), jnp.float32),
                        pltpu.VMEM((2 * DIL_LAG, 2 * c, 2 * c), jnp.bfloat16),
                        pltpu.VMEM((2 * DIL_LAG, 8, 2 * c), jnp.float32),
                        pltpu.VMEM((2 * DIL_LAG, 8, 2 * c), jnp.float32)],
        compiler_params=pltpu.CompilerParams(
            dimension_semantics=("arbitrary", "arbitrary"), vmem_limit_bytes=VMEM_LIMIT),
        name="dilated",
    )(*operands)


M_INIT = -1e30


def _moba_pair_tables(n_blocks):
    pairs = [(i, j) for i in range(1, n_blocks) for j in range(i)]
    pairs += [pairs[-1]] * MOBA_RING
    return np.array([p[0] for p in pairs], np.int32), np.array([p[1] for p in pairs], np.int32)


def _moba_kernel(slopes_ref, qblk_ref, kblk_ref, q0_ref, q1_ref, k0_ref, k1_ref, v0_ref, v1_ref, sel_ref,
                 o_ref, acc_ref, m_ref, l_ref, mask_ref, *rings):
    s_ref, p_ref, mloc_ref, stats_ref = (rings[n * MOBA_RING:(n + 1) * MOBA_RING] for n in range(4))
    pair = pl.program_id(0)
    bs = MOBA_BLOCK
    n_blocks = k0_ref.shape[1] // bs
    n_past = qblk_ref.shape[0] - MOBA_RING
    sel_rows = LANES // N_HEADS_B
    slopes = (slopes_ref[2 * pair], slopes_ref[2 * pair + 1])
    q_refs, k_refs, v_refs = (q0_ref, q1_ref), (k0_ref, k1_ref), (v0_ref, v1_ref)
    sum_row = (HEAD_DIM, 0)

    @pl.when(pl.program_id(1) == 0)
    def _():
        rel = (lax.broadcasted_iota(jnp.int32, (bs, bs), 0)
               - lax.broadcasted_iota(jnp.int32, (bs, bs), 1))
        mask_ref[...] = jnp.where(rel <= 0, 0.0, NEG_INF)

    m_ref[...] = jnp.full_like(m_ref, M_INIT)
    l_ref[...] = jnp.zeros_like(l_ref)
    acc_ref[...] = jnp.zeros_like(acc_ref)
    q_pos = lax.broadcasted_iota(jnp.int32, (1, bs), 1).astype(jnp.float32)

    def make_stages(blocks_of, own):
        def scores(t, slot):
            i, j = blocks_of(t)
            q_off = pl.multiple_of(i * bs, bs)
            k_off = pl.multiple_of(j * bs, bs)
            for h in range(2):
                st = _dot_nt(k_refs[h][0, pl.ds(k_off, bs), :], q_refs[h][0, pl.ds(q_off, bs), :])
                if own:
                    st = st + mask_ref[...]
                s_ref[slot][h] = st
                mloc_ref[slot][h:h + 1, :] = jnp.max(st, axis=0, keepdims=True)

        def softmax(t, slot):
            for h in range(2):
                m_loc = mloc_ref[slot][h:h + 1, :]
                stats_ref[slot][h:h + 1, :] = m_loc
                p_ref[slot][h] = jnp.exp2(s_ref[slot][h] - m_loc).astype(p_ref[slot].dtype)

        def merge(t, slot):
            i, j = blocks_of(t)
            q_off = pl.multiple_of(i * bs, bs)
            k_off = pl.multiple_of(j * bs, bs)
            gap = ((i - j) * bs).astype(jnp.float32)
            for h in range(2):
                pv = lax.dot_general(v_refs[h][0, pl.ds(k_off, bs), :], p_ref[slot][h],
                                     (((0,), (0,)), ((), ())), preferred_element_type=jnp.float32)
                out_t = pv[h * HEAD_DIM:(h + 1) * HEAD_DIM]
                l_blk = pv[sum_row[h]:sum_row[h] + 1]
                m_blk = stats_ref[slot][h:h + 1, :] - slopes[h] * (q_pos + gap)
                m_old = m_ref[h:h + 1, pl.ds(q_off, bs)]
                if own:
                    m_new = jnp.maximum(m_old, m_blk)
                    a_blk = jnp.exp2(m_blk - m_new)
                else:
                    chosen = sel_ref[0, pl.ds(h * sel_rows + j, 1), pl.ds(q_off, bs)] > 0.5
                    m_new = jnp.where(chosen, jnp.maximum(m_old, m_blk), m_old)
                    a_blk = jnp.where(chosen, jnp.exp2(m_blk - m_new), 0.0)
                a_old = jnp.exp2(m_old - m_new)
                acc_ref[h, :, pl.ds(q_off, bs)] = a_old * acc_ref[h, :, pl.ds(q_off, bs)] + a_blk * out_t
                l_ref[h:h + 1, pl.ds(q_off, bs)] = a_old * l_ref[h:h + 1, pl.ds(q_off, bs)] + a_blk * l_blk
                m_ref[h:h + 1, pl.ds(q_off, bs)] = m_new

        return scores, softmax, merge

    def own_blocks(t):
        i = jnp.minimum(t, n_blocks - 1)
        return i, i

    _software_pipeline(n_blocks, *make_stages(own_blocks, own=True), MOBA_OWN_LAG, scores_first=False)
    _software_pipeline(n_past, *make_stages(lambda t: (qblk_ref[t], kblk_ref[t]), own=False),
                       MOBA_LAG, scores_first=False)

    def finish(i, carry):
        q0 = pl.multiple_of(i * bs, bs)
        halves = [acc_ref[h, :, pl.ds(q0, bs)] / l_ref[h:h + 1, pl.ds(q0, bs)] for h in range(2)]
        o_ref[0, pl.ds(q0, bs), :] = jnp.concatenate(halves, axis=0).T.astype(o_ref.dtype)
        return carry

    lax.fori_loop(0, n_blocks, finish, 0)


def _moba(heads, sel, slopes):
    b, s, _ = heads.shape
    n_pairs = WIDTH_B // LANES
    sel_rows = 2 * (LANES // N_HEADS_B)
    qblk, kblk = _moba_pair_tables(s // MOBA_BLOCK)
    smem = pl.BlockSpec(memory_space=pltpu.SMEM)
    tile = lambda n: pl.BlockSpec((1, s, LANES), lambda p, i: (i, 0, 6 * p + n))
    return pl.pallas_call(
        _moba_kernel,
        grid=(n_pairs, b),
        in_specs=[smem, smem, smem] + [tile(n) for n in range(6)]
                 + [pl.BlockSpec((1, sel_rows, s), lambda p, i: (i, p, 0))],
        out_specs=pl.BlockSpec((1, s, LANES), lambda p, i: (i, 0, p)),
        out_shape=jax.ShapeDtypeStruct((b, s, WIDTH_B), jnp.float32),
        scratch_shapes=[pltpu.VMEM((2, HEAD_DIM, s), jnp.float32),
                        pltpu.VMEM((8, s), jnp.float32),
                        pltpu.VMEM((8, s), jnp.float32),
                        pltpu.VMEM((MOBA_BLOCK, MOBA_BLOCK), jnp.float32)]
                       + [pltpu.VMEM((2, MOBA_BLOCK, MOBA_BLOCK), jnp.float32)] * MOBA_RING
                       + [pltpu.VMEM((2, MOBA_BLOCK, MOBA_BLOCK), jnp.bfloat16)] * MOBA_RING
                       + [pltpu.VMEM((8, MOBA_BLOCK), jnp.float32)] * (2 * MOBA_RING),
        compiler_params=pltpu.CompilerParams(
            dimension_semantics=("arbitrary", "arbitrary"), vmem_limit_bytes=VMEM_LIMIT),
        name="moba",
    )(slopes, jnp.asarray(qblk), jnp.asarray(kblk), *([heads] * 6), sel)


def _rms(x, g):
    return (x * lax.rsqrt(jnp.mean(x * x, axis=-1, keepdims=True) + EPS)) * g


def _tail_kernel(x_ref, p_ref, oa_ref, ob_ref, ga_ref, gb_ref, wout_ref, gmlp_ref, wup_ref, wdown_ref,
                 gple_ref, wgate_ref, bgate_ref, wproj_ref, gfin_ref, o_ref):
    bf = jnp.bfloat16
    ya = _rms(oa_ref[0].astype(jnp.float32), ga_ref[...]).astype(bf)
    yb = _rms(ob_ref[0].astype(jnp.float32), gb_ref[...]).astype(bf)
    y = (jnp.dot(ya, wout_ref[0:WIDTH_A, :], preferred_element_type=jnp.float32)
         + jnp.dot(yb, wout_ref[WIDTH_A:, :], preferred_element_type=jnp.float32))
    h = x_ref[0] + y

    hn = _rms(h, gmlp_ref[...]).astype(bf)
    ff_chunk = 1024
    for c in range(D_FF // ff_chunk):
        u = jnp.dot(hn, wup_ref[:, c * ff_chunk:(c + 1) * ff_chunk], preferred_element_type=jnp.float32)
        u = jnp.square(jnp.maximum(u, 0.0)).astype(bf)
        h = h + jnp.dot(u, wdown_ref[c * ff_chunk:(c + 1) * ff_chunk, :], preferred_element_type=jnp.float32)

    z = jnp.dot(_rms(h, gple_ref[...]).astype(bf), wgate_ref[...], preferred_element_type=jnp.float32)
    gate = jax.nn.sigmoid(z + bgate_ref[...])
    h = h + gate * jnp.dot(p_ref[0].astype(bf), wproj_ref[...], preferred_element_type=jnp.float32)
    o_ref[0] = _rms(h, gfin_ref[...]).astype(o_ref.dtype)


def _tail(x, p, oa, ob, ga, gb, wout, gmlp, wup, wdown, gple, wgate, bgate, wproj, gfin):
    b, s, d = x.shape
    rows = lambda width: pl.BlockSpec((1, TAIL_ROWS, width), lambda i, t: (i, t, 0))
    full = lambda arr: pl.BlockSpec(arr.shape, lambda i, t: (0,) * arr.ndim,
                                    pipeline_mode=pl.Buffered(1))
    consts = (ga, gb, wout, gmlp, wup, wdown, gple, wgate, bgate, wproj, gfin)
    return pl.pallas_call(
        _tail_kernel,
        grid=(b, s // TAIL_ROWS),
        in_specs=[rows(d), rows(PLE_DIM), rows(WIDTH_A), rows(WIDTH_B)] + [full(a) for a in consts],
        out_specs=rows(d),
        out_shape=jax.ShapeDtypeStruct((b, s, d), x.dtype),
        compiler_params=pltpu.CompilerParams(
            dimension_semantics=("arbitrary", "arbitrary"), vmem_limit_bytes=VMEM_LIMIT),
        name="tail",
    )(x, p, oa, ob, *consts)


def kernel(x, p, g_attn, w_in, g_out_a, g_out_b, w_out, g_mlp, w_up, w_down, g_ple, w_ple_gate,
           b_ple_gate, w_ple_proj, g_final):
    assert x.shape[1:] == (4096, D_MODEL) and p.shape[0] == 1 and w_in.shape[0] == 1
    b, s, _ = x.shape
    bf = jnp.bfloat16
    slopes_a, slopes_b = _alibi_log2_slopes()
    row = lambda v: v.reshape(1, -1)

    qkv_a, gate, qkv_a4, qkv_a16, moba_heads = _project(x, row(g_attn[0]), w_in[0].astype(bf))
    sel = _select(gate)
    oa = _dilated([qkv_a, qkv_a4.reshape(b, s, 3 * WIDTH_A), qkv_a16.reshape(b, s, 3 * WIDTH_A)],
                  jnp.asarray(slopes_a))
    ob = _moba(moba_heads, sel, jnp.asarray(slopes_b))

    return _tail(x, p[0], oa, ob, row(g_out_a[0]), row(g_out_b[0]), w_out[0].astype(bf), row(g_mlp[0]),
                 w_up[0].astype(bf), w_down[0].astype(bf), row(g_ple[0]), w_ple_gate[0].astype(bf),
                 row(b_ple_gate[0]), w_ple_proj[0].astype(bf), row(g_final))
```

```python
import functools
import math

import jax
import jax.numpy as jnp
import numpy as np
from jax import lax
from jax.experimental import pallas as pl
from jax.experimental.pallas import tpu as pltpu

D_MODEL = 1024
HEAD_DIM = 64
N_HEADS = D_MODEL // HEAD_DIM
N_HEADS_A = N_HEADS // 2
N_HEADS_B = N_HEADS - N_HEADS_A
WIDTH_A = N_HEADS_A * HEAD_DIM
WIDTH_B = N_HEADS_B * HEAD_DIM
DILATED_PAIRS = ((128, 1), (512, 4), (2048, 16))
MOBA_BLOCK = 256
MOBA_TOPK = 3
D_FF = 4 * D_MODEL
PLE_DIM = 256
EPS = 1e-6

LANES = 128
LOG2E = 1.4426950408889634
Q_SCALE = HEAD_DIM ** -0.5 * LOG2E
NEG_INF = float("-inf")
VMEM_LIMIT = 56 * 1024 * 1024

PROJ_ROWS = 512
TAIL_ROWS = 512
DIL_BLOCK = 128


def _alibi_log2_slopes():
    s = np.array([2.0 ** (-8.0 * (i + 1) / N_HEADS) for i in range(N_HEADS)], dtype=np.float64)
    return (s[0::2] * LOG2E).astype(np.float32), (s[1::2] * LOG2E).astype(np.float32)


def _dot_nt(a, b):
    return lax.dot_general(a, b, (((1,), (1,)), ((), ())), preferred_element_type=jnp.float32)


def _split_bf16(x):
    hi = x.astype(jnp.bfloat16)
    lo = (x - hi.astype(jnp.float32)).astype(jnp.bfloat16)
    return hi, lo


DIL_LAG = 4
MOBA_OWN_LAG = 4
MOBA_LAG = 6
MOBA_RING = 2 * MOBA_LAG


def _software_pipeline(n_steps, scores, softmax, emit, lag, scores_first=True):
    ring = 2 * lag
    assert n_steps % ring == 0
    for t in range(ring):
        scores(t, t)
    for t in range(lag):
        softmax(t, t)

    def body(it, carry, last=False):
        for k in range(ring):
            t = ring * it + k
            if scores_first and not last:
                scores(t + ring, k)
            if not last or k < lag:
                softmax(t + lag, (k + lag) % ring)
            emit(t, k)
            if not scores_first and not last:
                scores(t + ring, k)
        return carry

    lax.fori_loop(0, n_steps // ring - 1, body, 0)
    body(n_steps // ring - 1, 0, last=True)


def _moba_slope_split():
    _, slopes = _alibi_log2_slopes()
    hi = slopes.astype(jnp.bfloat16).astype(np.float32)
    lo = (slopes - hi).astype(jnp.bfloat16).astype(np.float32)
    return hi, lo


def _proj_kernel(x_ref, g_ref, w_ref, qkv_ref, gate_ref, a4_ref, a16_ref, moba_ref, kmean_ref, slab_ref,
                 *, slope_hi, slope_lo):
    t = pl.program_id(1)
    x = x_ref[0]
    ms = jnp.mean(x * x, axis=-1, keepdims=True)
    hn = (x * lax.rsqrt(ms + EPS)) * g_ref[...]
    qkv = jnp.dot(hn.astype(jnp.bfloat16), w_ref[...], preferred_element_type=jnp.float32)

    qa = qkv[:, 0:WIDTH_A]
    qb = qkv[:, 3 * WIDTH_A:3 * WIDTH_A + WIDTH_B]
    kb = qkv[:, 3 * WIDTH_A + WIDTH_B:3 * WIDTH_A + 2 * WIDTH_B]
    vb = qkv[:, 3 * WIDTH_A + 2 * WIDTH_B:]
    qkv_ref[0, :, 0:WIDTH_A] = (qa * Q_SCALE).astype(qkv_ref.dtype)
    qkv_ref[0, :, WIDTH_A:] = qkv[:, WIDTH_A:3 * WIDTH_A].astype(qkv_ref.dtype)

    lane = lax.broadcasted_iota(jnp.int32, (PROJ_ROWS, LANES), 1)
    key_pos = jnp.bitwise_and(lax.broadcasted_iota(jnp.int32, (PROJ_ROWS, LANES), 0),
                              MOBA_BLOCK - 1).astype(jnp.float32)
    for pair in range(WIDTH_B // LANES):
        cols = slice(pair * LANES, (pair + 1) * LANES)
        q_t, k_t, v_t = qb[:, cols] * Q_SCALE, kb[:, cols], vb[:, cols]
        for h in range(2):
            own = (lane < HEAD_DIM) if h == 0 else (lane >= HEAD_DIM)
            f0 = HEAD_DIM if h == 0 else 0
            head = 2 * pair + h
            q_feat = jnp.where(lane == f0, float(slope_hi[head]),
                               jnp.where(lane == f0 + 1, float(slope_lo[head]), 0.0))
            k_feat = jnp.where((lane == f0) | (lane == f0 + 1), key_pos, 0.0)
            base = (pair * 6 + h) * LANES
            for n, tile in enumerate((jnp.where(own, q_t, q_feat), jnp.where(own, k_t, k_feat),
                                      jnp.where(own, v_t, 1.0))):
                moba_ref[0, :, base + 2 * n * LANES:base + (2 * n + 1) * LANES] = tile.astype(moba_ref.dtype)

    rows4 = PROJ_ROWS // 4
    rows16 = PROJ_ROWS // 16
    for cb in range(3 * WIDTH_A // LANES):
        cols = slice(cb * LANES, (cb + 1) * LANES)
        val = qkv[:, cols]
        nat, res4 = 2 * (cb % 2), 2 * (cb % 2) + 1
        slab_ref[nat] = val * Q_SCALE if cb < WIDTH_A // LANES else val
        for r4 in range(4):
            part = slab_ref[nat, pl.ds(r4, rows4, stride=4), :]
            a4_ref[0, r4, :, cols] = part.astype(a4_ref.dtype)
            slab_ref[res4, r4 * rows4:(r4 + 1) * rows4, :] = part
        for r4 in range(4):
            for j in range(4):
                part = slab_ref[res4, pl.ds(r4 * rows4 + j, rows16, stride=4), :]
                a16_ref[0, 4 * j + r4, :, cols] = part.astype(a16_ref.dtype)

    @pl.when(t == 0)
    def _():
        kmean_ref[...] = jnp.zeros_like(kmean_ref)

    blocks_per_tile = PROJ_ROWS // MOBA_BLOCK
    for sb in range(blocks_per_tile):
        km = jnp.mean(kb[sb * MOBA_BLOCK:(sb + 1) * MOBA_BLOCK, :], axis=0, keepdims=True)
        kmean_ref[pl.ds(t * blocks_per_tile + sb, 1), :] = km

    kmean = kmean_ref[...]
    lane_head = lax.broadcasted_iota(jnp.int32, kmean.shape, 1) // HEAD_DIM
    wt = jnp.concatenate(
        [jnp.where(lane_head == h, kmean, 0.0) for h in range(N_HEADS_B)], axis=0)
    q_hi, q_lo = _split_bf16(qb)
    w_hi, w_lo = _split_bf16(wt)
    gate_ref[0] = _dot_nt(w_hi, q_hi) + (_dot_nt(w_lo, q_hi) + _dot_nt(w_hi, q_lo))


def _project(x, g_attn, w_in_bf16):
    b, s, d = x.shape
    n_blocks = s // MOBA_BLOCK
    slope_hi, slope_lo = _moba_slope_split()
    return pl.pallas_call(
        functools.partial(_proj_kernel, slope_hi=slope_hi, slope_lo=slope_lo),
        grid=(b, s // PROJ_ROWS),
        in_specs=[
            pl.BlockSpec((1, PROJ_ROWS, d), lambda i, t: (i, t, 0)),
            pl.BlockSpec((1, d), lambda i, t: (0, 0)),
            pl.BlockSpec((d, 3 * d), lambda i, t: (0, 0)),
        ],
        out_specs=[
            pl.BlockSpec((1, PROJ_ROWS, 3 * WIDTH_A), lambda i, t: (i, t, 0)),
            pl.BlockSpec((1, LANES, PROJ_ROWS), lambda i, t: (i, 0, t)),
            pl.BlockSpec((1, 4, PROJ_ROWS // 4, 3 * WIDTH_A), lambda i, t: (i, 0, t, 0)),
            pl.BlockSpec((1, 16, PROJ_ROWS // 16, 3 * WIDTH_A), lambda i, t: (i, 0, t, 0)),
            pl.BlockSpec((1, PROJ_ROWS, 6 * WIDTH_B), lambda i, t: (i, t, 0)),
        ],
        out_shape=[
            jax.ShapeDtypeStruct((b, s, 3 * WIDTH_A), jnp.bfloat16),
            jax.ShapeDtypeStruct((b, LANES, s), jnp.float32),
            jax.ShapeDtypeStruct((b, 4, s // 4, 3 * WIDTH_A), jnp.bfloat16),
            jax.ShapeDtypeStruct((b, 16, s // 16, 3 * WIDTH_A), jnp.bfloat16),
            jax.ShapeDtypeStruct((b, s, 6 * WIDTH_B), jnp.bfloat16),
        ],
        scratch_shapes=[pltpu.VMEM((n_blocks, WIDTH_B), jnp.float32),
                        pltpu.VMEM((4, PROJ_ROWS, LANES), jnp.float32)],
        compiler_params=pltpu.CompilerParams(
            dimension_semantics=("arbitrary", "arbitrary"), vmem_limit_bytes=VMEM_LIMIT),
        name="proj",
    )(x, g_attn, w_in_bf16)


def _select_kernel(gate_ref, sel_ref):
    n_blocks = LANES // N_HEADS_B
    width = gate_ref.shape[2]
    q_pos = pl.program_id(1) * width + lax.broadcasted_iota(jnp.int32, (n_blocks, width), 1)
    blk = lax.broadcasted_iota(jnp.int32, (n_blocks, width), 0)
    past = blk < q_pos // MOBA_BLOCK
    for h in range(N_HEADS_B):
        g = jnp.where(past, gate_ref[0, h * n_blocks:(h + 1) * n_blocks, :], NEG_INF)
        rank = jnp.zeros(g.shape, jnp.int32)
        for other_blk in range(n_blocks):
            other = g[other_blk:other_blk + 1, :]
            beats = (other > g) | ((other == g) & (other_blk < blk))
            rank = rank + beats.astype(jnp.int32)
        sel_ref[0, h * n_blocks:(h + 1) * n_blocks, :] = (past & (rank < MOBA_TOPK)).astype(sel_ref.dtype)


def _select(gate):
    b, _, s = gate.shape
    width = 1024
    return pl.pallas_call(
        _select_kernel,
        grid=(b, s // width),
        in_specs=[pl.BlockSpec((1, LANES, width), lambda i, t: (i, 0, t))],
        out_specs=pl.BlockSpec((1, LANES, width), lambda i, t: (i, 0, t)),
        out_shape=jax.ShapeDtypeStruct((b, LANES, s), jnp.float32),
        compiler_params=pltpu.CompilerParams(dimension_semantics=("arbitrary", "arbitrary")),
        name="select",
    )(gate)


def _dilated_kernel(slopes_ref, q1_ref, k1_ref, v1_ref, q4_ref, k4_ref, v4_ref,
                    q16_ref, k16_ref, v16_ref, o_ref,
                    acc_ref, stat_ref, bias_ref, s_ref, p_ref, mloc_ref, stats_ref):
    pair = pl.program_id(0)
    seq = k1_ref.shape[1]
    c = DIL_BLOCK
    n_steps = seq // c
    dilations = tuple(d for _, d in DILATED_PAIRS)

    @pl.when(pl.program_id(1) == 0)
    def _():
        key = lax.broadcasted_iota(jnp.int32, (2 * c, 2 * c), 0)
        lane = lax.broadcasted_iota(jnp.int32, (2 * c, 2 * c), 1)
        slope_l = jnp.where(lane >= c, slopes_ref[2 * pair + 1], slopes_ref[2 * pair])
        for b_idx, dil in enumerate(dilations):
            for first in range(2):
                dist = (0 if first else c) + jnp.bitwise_and(lane, c - 1) - key
                bias_ref[2 * b_idx + first] = jnp.where(
                    (dist >= 0) & (dist <= c), (-dil) * slope_l * dist.astype(jnp.float32), NEG_INF)

    head0 = lax.broadcasted_iota(jnp.int32, (c, LANES), 1) < HEAD_DIM

    branches = ((q1_ref, k1_ref, v1_ref), (q4_ref, k4_ref, v4_ref), (q16_ref, k16_ref, v16_ref))
    for b_idx, (dil, (q_ref, k_ref, v_ref)) in enumerate(zip(dilations, branches)):
        sub_len = seq // dil
        n_chunks = sub_len // c

        def offsets(t, n_chunks=n_chunks, sub_len=sub_len):
            t = jnp.asarray(t, jnp.int32)
            r = t // n_chunks
            n = t % n_chunks
            q_off = pl.multiple_of(r * sub_len + n * c, c)
            k_off = pl.multiple_of(r * sub_len + jnp.maximum(n - 1, 0) * c, c)
            return r, n, q_off, k_off

        def scores(t, slot, b_idx=b_idx, q_ref=q_ref, k_ref=k_ref, offsets=offsets):
            _, n, q_off, k_off = offsets(t)
            q = q_ref[0, pl.ds(q_off, c), :]
            q_heads = jnp.concatenate([jnp.where(head0, q, jnp.zeros_like(q)),
                                       jnp.where(head0, jnp.zeros_like(q), q)], axis=0)
            keys = k_ref[0, pl.ds(k_off, 2 * c), :]
            first = (n == 0).astype(jnp.int32)
            st = _dot_nt(keys, q_heads) + bias_ref[2 * b_idx + first]
            s_ref[slot] = st
            mloc_ref[slot, 0:1, :] = jnp.max(st, axis=0, keepdims=True)

        def softmax(t, slot):
            m_loc = mloc_ref[slot, 0:1, :]
            p = jnp.exp2(s_ref[slot] - m_loc)
            stats_ref[slot, 0:1, :] = m_loc
            stats_ref[slot, 1:2, :] = jnp.sum(p, axis=0, keepdims=True)
            p_ref[slot] = p.astype(p_ref.dtype)

        def emit(t, slot, b_idx=b_idx, dil=dil, v_ref=v_ref, offsets=offsets):
            r, n, _, k_off = offsets(t)
            vals = v_ref[0, pl.ds(k_off, 2 * c), :]
            pv = lax.dot_general(vals, p_ref[slot], (((0,), (0,)), ((), ())),
                                 preferred_element_type=jnp.float32)
            out_t = jnp.concatenate([pv[:HEAD_DIM, :c], pv[HEAD_DIM:, c:]], axis=0)
            if dil == 1:
                rows = pl.ds(pl.multiple_of(n * c, c), c)
            else:
                rows = pl.ds(r + dil * c * n, c, stride=dil)
            l_loc = stats_ref[slot, 1:2, :]
            inv_l = 1.0 / l_loc
            lse = stats_ref[slot, 0:1, :] + jnp.log2(l_loc)
            out_t = out_t * jnp.concatenate([jnp.broadcast_to(inv_l[:, :c], (HEAD_DIM, c)),
                                             jnp.broadcast_to(inv_l[:, c:], (HEAD_DIM, c))], axis=0)
            lse_t = jnp.concatenate([jnp.broadcast_to(lse[:, :c], (HEAD_DIM, c)),
                                     jnp.broadcast_to(lse[:, c:], (HEAD_DIM, c))], axis=0)
            acc_ref[b_idx, rows, :] = out_t.T
            stat_ref[b_idx, rows, :] = lse_t.T

        _software_pipeline(n_steps, scores, softmax, emit, DIL_LAG)

    def merge(ci, carry):
        rows = pl.ds(pl.multiple_of(ci * c, c), c)
        lses = [stat_ref[b_idx, rows, :] for b_idx in range(len(dilations))]
        lse_max = functools.reduce(jnp.maximum, lses)
        weights = [jnp.exp2(lse - lse_max) for lse in lses]
        out = functools.reduce(lambda a, b: a + b,
                               [w * acc_ref[b_idx, rows, :] for b_idx, w in enumerate(weights)])
        o_ref[0, rows, :] = (out / functools.reduce(lambda a, b: a + b, weights)).astype(o_ref.dtype)
        return carry

    lax.fori_loop(0, seq // c, merge, 0)


def _dilated(qkv_by_dilation, slopes):
    b, s, _ = qkv_by_dilation[0].shape
    n_pairs = WIDTH_A // LANES
    tok_major = pl.BlockSpec((1, s, LANES), lambda p, i: (i, 0, p))
    operands, in_specs = [slopes], [pl.BlockSpec(memory_space=pltpu.SMEM)]
    for arr in qkv_by_dilation:
        for part in range(3):
            operands.append(arr)
            in_specs.append(pl.BlockSpec((1, s, LANES), lambda p, i, part=part: (i, 0, part * n_pairs + p)))
    n_br = len(DILATED_PAIRS)
    c = DIL_BLOCK
    return pl.pallas_call(
        _dilated_kernel,
        grid=(n_pairs, b),
        in_specs=in_specs,
        out_specs=tok_major,
        out_shape=jax.ShapeDtypeStruct((b, s, WIDTH_A), jnp.float32),
        scratch_shapes=[pltpu.VMEM((n_br, s, LANES), jnp.float32),
                        pltpu.VMEM((n_br, s, LANES), jnp.float32),
                        pltpu.VMEM((2 * n_br, 2 * c, 2 * c), jnp.float32),
                        pltpu.VMEM((2 * DIL_LAG, 2 * c, 2 * c), jnp.float32),
                        pltpu.VMEM((2 * DIL_LAG, 2 * c, 2 * c), jnp.bfloat16),
                        pltpu.VMEM((2 * DIL_LAG, 8, 2 * c), jnp.float32),
                        pltpu.VMEM((2 * DIL_LAG, 8, 2 * c), jnp.float32)],
        compiler_params=pltpu.CompilerParams(
            dimension_semantics=("arbitrary", "arbitrary"), vmem_limit_bytes=VMEM_LIMIT),
        name="dilated",
    )(*operands)


M_INIT = -1e30


def _moba_pair_tables(n_blocks):
    pairs = [(i, j) for i in range(1, n_blocks) for j in range(i)]
    return np.array([p[0] for p in pairs], np.int32), np.array([p[1] for p in pairs], np.int32)


def _moba_kernel(slopes_ref, qblk_ref, kblk_ref, q0_ref, q1_ref, k0_ref, k1_ref, v0_ref, v1_ref, sel_ref,
                 o_ref, acc_ref, m_ref, l_ref, mask_ref, *rings):
    s_ref, p_ref, mloc_ref, stats_ref = (rings[n * MOBA_RING:(n + 1) * MOBA_RING] for n in range(4))
    pair = pl.program_id(0)
    bs = MOBA_BLOCK
    n_blocks = k0_ref.shape[1] // bs
    n_past = qblk_ref.shape[0]
    sel_rows = LANES // N_HEADS_B
    slopes = (slopes_ref[2 * pair], slopes_ref[2 * pair + 1])
    q_refs, k_refs, v_refs = (q0_ref, q1_ref), (k0_ref, k1_ref), (v0_ref, v1_ref)
    sum_row = (HEAD_DIM, 0)

    @pl.when(pl.program_id(1) == 0)
    def _():
        rel = (lax.broadcasted_iota(jnp.int32, (bs, bs), 0)
               - lax.broadcasted_iota(jnp.int32, (bs, bs), 1))
        mask_ref[...] = jnp.where(rel <= 0, 0.0, NEG_INF)

    m_ref[...] = jnp.full_like(m_ref, M_INIT)
    l_ref[...] = jnp.zeros_like(l_ref)
    acc_ref[...] = jnp.zeros_like(acc_ref)
    q_pos = lax.broadcasted_iota(jnp.int32, (1, bs), 1).astype(jnp.float32)

    def make_stages(blocks_of, own):
        def scores(t, slot):
            i, j = blocks_of(t)
            q_off = pl.multiple_of(i * bs, bs)
            k_off = pl.multiple_of(j * bs, bs)
            for h in range(2):
                st = _dot_nt(k_refs[h][0, pl.ds(k_off, bs), :], q_refs[h][0, pl.ds(q_off, bs), :])
                if own:
                    st = st + mask_ref[...]
                s_ref[slot][h] = st
                mloc_ref[slot][h:h + 1, :] = jnp.max(st, axis=0, keepdims=True)

        def softmax(t, slot):
            for h in range(2):
                m_loc = mloc_ref[slot][h:h + 1, :]
                stats_ref[slot][h:h + 1, :] = m_loc
                p_ref[slot][h] = jnp.exp2(s_ref[slot][h] - m_loc).astype(p_ref[slot].dtype)

        def merge(t, slot):
            i, j = blocks_of(t)
            q_off = pl.multiple_of(i * bs, bs)
            k_off = pl.multiple_of(j * bs, bs)
            gap = ((i - j) * bs).astype(jnp.float32)
            for h in range(2):
                pv = lax.dot_general(v_refs[h][0, pl.ds(k_off, bs), :], p_ref[slot][h],
                                     (((0,), (0,)), ((), ())), preferred_element_type=jnp.float32)
                out_t = pv[h * HEAD_DIM:(h + 1) * HEAD_DIM]
                l_blk = pv[sum_row[h]:sum_row[h] + 1]
                m_blk = stats_ref[slot][h:h + 1, :] - slopes[h] * (q_pos + gap)
                m_old = m_ref[h:h + 1, pl.ds(q_off, bs)]
                if own:
                    m_new = jnp.maximum(m_old, m_blk)
                    a_blk = jnp.exp2(m_blk - m_new)
                else:
                    chosen = sel_ref[0, pl.ds(h * sel_rows + j, 1), pl.ds(q_off, bs)] > 0.5
                    m_new = jnp.where(chosen, jnp.maximum(m_old, m_blk), m_old)
                    a_blk = jnp.where(chosen, jnp.exp2(m_blk - m_new), 0.0)
                a_old = jnp.exp2(m_old - m_new)
                acc_ref[h, :, pl.ds(q_off, bs)] = a_old * acc_ref[h, :, pl.ds(q_off, bs)] + a_blk * out_t
                l_ref[h:h + 1, pl.ds(q_off, bs)] = a_old * l_ref[h:h + 1, pl.ds(q_off, bs)] + a_blk * l_blk
                m_ref[h:h + 1, pl.ds(q_off, bs)] = m_new

        return scores, softmax, merge

    def own_blocks(t):
        i = jnp.asarray(t, jnp.int32)
        return i, i

    _software_pipeline(n_blocks, *make_stages(own_blocks, own=True), MOBA_OWN_LAG, scores_first=False)
    _software_pipeline(n_past, *make_stages(lambda t: (qblk_ref[t], kblk_ref[t]), own=False),
                       MOBA_LAG, scores_first=False)

    def finish(i, carry):
        q0 = pl.multiple_of(i * bs, bs)
        halves = [acc_ref[h, :, pl.ds(q0, bs)] / l_ref[h:h + 1, pl.ds(q0, bs)] for h in range(2)]
        o_ref[0, pl.ds(q0, bs), :] = jnp.concatenate(halves, axis=0).T.astype(o_ref.dtype)
        return carry

    lax.fori_loop(0, n_blocks, finish, 0)


def _moba(heads, sel, slopes):
    b, s, _ = heads.shape
    n_pairs = WIDTH_B // LANES
    sel_rows = 2 * (LANES // N_HEADS_B)
    qblk, kblk = _moba_pair_tables(s // MOBA_BLOCK)
    smem = pl.BlockSpec(memory_space=pltpu.SMEM)
    tile = lambda n: pl.BlockSpec((1, s, LANES), lambda p, i: (i, 0, 6 * p + n))
    return pl.pallas_call(
        _moba_kernel,
        grid=(n_pairs, b),
        in_specs=[smem, smem, smem] + [tile(n) for n in range(6)]
                 + [pl.BlockSpec((1, sel_rows, s), lambda p, i: (i, p, 0))],
        out_specs=pl.BlockSpec((1, s, LANES), lambda p, i: (i, 0, p)),
        out_shape=jax.ShapeDtypeStruct((b, s, WIDTH_B), jnp.float32),
        scratch_shapes=[pltpu.VMEM((2, HEAD_DIM, s), jnp.float32),
                        pltpu.VMEM((8, s), jnp.float32),
                        pltpu.VMEM((8, s), jnp.float32),
                        pltpu.VMEM((MOBA_BLOCK, MOBA_BLOCK), jnp.float32)]
                       + [pltpu.VMEM((2, MOBA_BLOCK, MOBA_BLOCK), jnp.float32)] * MOBA_RING
                       + [pltpu.VMEM((2, MOBA_BLOCK, MOBA_BLOCK), jnp.bfloat16)] * MOBA_RING
                       + [pltpu.VMEM((8, MOBA_BLOCK), jnp.float32)] * (2 * MOBA_RING),
        compiler_params=pltpu.CompilerParams(
            dimension_semantics=("arbitrary", "arbitrary"), vmem_limit_bytes=VMEM_LIMIT),
        name="moba",
    )(slopes, jnp.asarray(qblk), jnp.asarray(kblk), *([heads] * 6), sel)


def _rms(x, g):
    return (x * lax.rsqrt(jnp.mean(x * x, axis=-1, keepdims=True) + EPS)) * g


def _tail_kernel(x_ref, p_ref, oa_ref, ob_ref, ga_ref, gb_ref, wout_ref, gmlp_ref, wup_ref, wdown_ref,
                 gple_ref, wgate_ref, bgate_ref, wproj_ref, gfin_ref, o_ref):
    bf = jnp.bfloat16
    ya = _rms(oa_ref[0].astype(jnp.float32), ga_ref[...]).astype(bf)
    yb = _rms(ob_ref[0].astype(jnp.float32), gb_ref[...]).astype(bf)
    y = (jnp.dot(ya, wout_ref[0:WIDTH_A, :], preferred_element_type=jnp.float32)
         + jnp.dot(yb, wout_ref[WIDTH_A:, :], preferred_element_type=jnp.float32))
    h = x_ref[0] + y

    hn = _rms(h, gmlp_ref[...]).astype(bf)
    ff_chunk = 1024
    for c in range(D_FF // ff_chunk):
        u = jnp.dot(hn, wup_ref[:, c * ff_chunk:(c + 1) * ff_chunk], preferred_element_type=jnp.float32)
        u = jnp.square(jnp.maximum(u, 0.0)).astype(bf)
        h = h + jnp.dot(u, wdown_ref[c * ff_chunk:(c + 1) * ff_chunk, :], preferred_element_type=jnp.float32)

    z = jnp.dot(_rms(h, gple_ref[...]).astype(bf), wgate_ref[...], preferred_element_type=jnp.float32)
    gate = jax.nn.sigmoid(z + bgate_ref[...])
    h = h + gate * jnp.dot(p_ref[0].astype(bf), wproj_ref[...], preferred_element_type=jnp.float32)
    o_ref[0] = _rms(h, gfin_ref[...]).astype(o_ref.dtype)


def _tail(x, p, oa, ob, ga, gb, wout, gmlp, wup, wdown, gple, wgate, bgate, wproj, gfin):
    b, s, d = x.shape
    rows = lambda width: pl.BlockSpec((1, TAIL_ROWS, width), lambda i, t: (i, t, 0))
    full = lambda arr: pl.BlockSpec(arr.shape, lambda i, t: (0,) * arr.ndim,
                                    pipeline_mode=pl.Buffered(1))
    consts = (ga, gb, wout, gmlp, wup, wdown, gple, wgate, bgate, wproj, gfin)
    return pl.pallas_call(
        _tail_kernel,
        grid=(b, s // TAIL_ROWS),
        in_specs=[rows(d), rows(PLE_DIM), rows(WIDTH_A), rows(WIDTH_B)] + [full(a) for a in consts],
        out_specs=rows(d),
        out_shape=jax.ShapeDtypeStruct((b, s, d), x.dtype),
        compiler_params=pltpu.CompilerParams(
            dimension_semantics=("arbitrary", "arbitrary"), vmem_limit_bytes=VMEM_LIMIT),
        name="tail",
    )(x, p, oa, ob, *consts)


def kernel(x, p, g_attn, w_in, g_out_a, g_out_b, w_out, g_mlp, w_up, w_down, g_ple, w_ple_gate,
           b_ple_gate, w_ple_proj, g_final):
    assert x.shape[1:] == (4096, D_MODEL) and p.shape[0] == 1 and w_in.shape[0] == 1
    b, s, _ = x.shape
    bf = jnp.bfloat16
    slopes_a, slopes_b = _alibi_log2_slopes()
    row = lambda v: v.reshape(1, -1)

    qkv_a, gate, qkv_a4, qkv_a16, moba_heads = _project(x, row(g_attn[0]), w_in[0].astype(bf))
    sel = _select(gate)
    oa = _dilated([qkv_a, qkv_a4.reshape(b, s, 3 * WIDTH_A), qkv_a16.reshape(b, s, 3 * WIDTH_A)],
                  jnp.asarray(slopes_a))
    ob = _moba(moba_heads, sel, jnp.asarray(slopes_b))

    return _tail(x, p[0], oa, ob, row(g_out_a[0]), row(g_out_b[0]), w_out[0].astype(bf), row(g_mlp[0]),
                 w_up[0].astype(bf), w_down[0].astype(bf), row(g_ple[0]), w_ple_gate[0].astype(bf),
                 row(b_ple_gate[0]), w_ple_proj[0].astype(bf), row(g_final))
```

```python
import functools
import math

import jax
import jax.numpy as jnp
import numpy as np
from jax import lax
from jax.experimental import pallas as pl
from jax.experimental.pallas import tpu as pltpu

D_MODEL = 1024
HEAD_DIM = 64
N_HEADS = D_MODEL // HEAD_DIM
N_HEADS_A = N_HEADS // 2
N_HEADS_B = N_HEADS - N_HEADS_A
WIDTH_A = N_HEADS_A * HEAD_DIM
WIDTH_B = N_HEADS_B * HEAD_DIM
DILATED_PAIRS = ((128, 1), (512, 4), (2048, 16))
MOBA_BLOCK = 256
MOBA_TOPK = 3
D_FF = 4 * D_MODEL
PLE_DIM = 256
EPS = 1e-6

LANES = 128
LOG2E = 1.4426950408889634
Q_SCALE = HEAD_DIM ** -0.5 * LOG2E
NEG_INF = float("-inf")
VMEM_LIMIT = 56 * 1024 * 1024

PROJ_ROWS = 512
TAIL_ROWS = 512
DIL_BLOCK = 128


def _alibi_log2_slopes():
    s = np.array([2.0 ** (-8.0 * (i + 1) / N_HEADS) for i in range(N_HEADS)], dtype=np.float64)
    return (s[0::2] * LOG2E).astype(np.float32), (s[1::2] * LOG2E).astype(np.float32)


def _dot_nt(a, b):
    return lax.dot_general(a, b, (((1,), (1,)), ((), ())), preferred_element_type=jnp.float32)


def _split_bf16(x):
    hi = x.astype(jnp.bfloat16)
    lo = (x - hi.astype(jnp.float32)).astype(jnp.bfloat16)
    return hi, lo


DIL_LAG = 4
MOBA_OWN_LAG = 4
MOBA_LAG = 10
MOBA_RING = 2 * MOBA_LAG


def _software_pipeline(n_steps, scores, softmax, emit, lag, scores_first=True):
    ring = 2 * lag
    assert n_steps % ring == 0
    for t in range(ring):
        scores(t, t)
    for t in range(lag):
        softmax(t, t)

    def body(it, carry, last=False):
        for k in range(ring):
            t = ring * it + k
            if scores_first and not last:
                scores(t + ring, k)
            if not last or k < lag:
                softmax(t + lag, (k + lag) % ring)
            emit(t, k)
            if not scores_first and not last:
                scores(t + ring, k)
        return carry

    lax.fori_loop(0, n_steps // ring - 1, body, 0)
    body(n_steps // ring - 1, 0, last=True)


def _moba_slope_split():
    _, slopes = _alibi_log2_slopes()
    hi = slopes.astype(jnp.bfloat16).astype(np.float32)
    lo = (slopes - hi).astype(jnp.bfloat16).astype(np.float32)
    return hi, lo


def _proj_kernel(x_ref, g_ref, w_ref, qkv_ref, gate_ref, a4_ref, a16_ref, moba_ref, kmean_ref, slab_ref,
                 *, slope_hi, slope_lo):
    t = pl.program_id(1)
    x = x_ref[0]
    ms = jnp.mean(x * x, axis=-1, keepdims=True)
    hn = (x * lax.rsqrt(ms + EPS)) * g_ref[...]
    qkv = jnp.dot(hn.astype(jnp.bfloat16), w_ref[...], preferred_element_type=jnp.float32)

    qa = qkv[:, 0:WIDTH_A]
    qb = qkv[:, 3 * WIDTH_A:3 * WIDTH_A + WIDTH_B]
    kb = qkv[:, 3 * WIDTH_A + WIDTH_B:3 * WIDTH_A + 2 * WIDTH_B]
    vb = qkv[:, 3 * WIDTH_A + 2 * WIDTH_B:]
    qkv_ref[0, :, 0:WIDTH_A] = (qa * Q_SCALE).astype(qkv_ref.dtype)
    qkv_ref[0, :, WIDTH_A:] = qkv[:, WIDTH_A:3 * WIDTH_A].astype(qkv_ref.dtype)

    lane = lax.broadcasted_iota(jnp.int32, (PROJ_ROWS, LANES), 1)
    key_pos = jnp.bitwise_and(lax.broadcasted_iota(jnp.int32, (PROJ_ROWS, LANES), 0),
                              MOBA_BLOCK - 1).astype(jnp.float32)
    for pair in range(WIDTH_B // LANES):
        cols = slice(pair * LANES, (pair + 1) * LANES)
        q_t, k_t, v_t = qb[:, cols] * Q_SCALE, kb[:, cols], vb[:, cols]
        for h in range(2):
            own = (lane < HEAD_DIM) if h == 0 else (lane >= HEAD_DIM)
            f0 = HEAD_DIM if h == 0 else 0
            head = 2 * pair + h
            q_feat = jnp.where(lane == f0, float(slope_hi[head]),
                               jnp.where(lane == f0 + 1, float(slope_lo[head]), 0.0))
            k_feat = jnp.where((lane == f0) | (lane == f0 + 1), key_pos, 0.0)
            base = (pair * 6 + h) * LANES
            for n, tile in enumerate((jnp.where(own, q_t, q_feat), jnp.where(own, k_t, k_feat),
                                      jnp.where(own, v_t, 1.0))):
                moba_ref[0, :, base + 2 * n * LANES:base + (2 * n + 1) * LANES] = tile.astype(moba_ref.dtype)

    rows4 = PROJ_ROWS // 4
    rows16 = PROJ_ROWS // 16
    for cb in range(3 * WIDTH_A // LANES):
        cols = slice(cb * LANES, (cb + 1) * LANES)
        val = qkv[:, cols]
        nat, res4 = 2 * (cb % 2), 2 * (cb % 2) + 1
        slab_ref[nat] = val * Q_SCALE if cb < WIDTH_A // LANES else val
        for r4 in range(4):
            part = slab_ref[nat, pl.ds(r4, rows4, stride=4), :]
            a4_ref[0, r4, :, cols] = part.astype(a4_ref.dtype)
            slab_ref[res4, r4 * rows4:(r4 + 1) * rows4, :] = part
        for r4 in range(4):
            for j in range(4):
                part = slab_ref[res4, pl.ds(r4 * rows4 + j, rows16, stride=4), :]
                a16_ref[0, 4 * j + r4, :, cols] = part.astype(a16_ref.dtype)

    @pl.when(t == 0)
    def _():
        kmean_ref[...] = jnp.zeros_like(kmean_ref)

    blocks_per_tile = PROJ_ROWS // MOBA_BLOCK
    for sb in range(blocks_per_tile):
        km = jnp.mean(kb[sb * MOBA_BLOCK:(sb + 1) * MOBA_BLOCK, :], axis=0, keepdims=True)
        kmean_ref[pl.ds(t * blocks_per_tile + sb, 1), :] = km

    kmean = kmean_ref[...]
    lane_head = lax.broadcasted_iota(jnp.int32, kmean.shape, 1) // HEAD_DIM
    wt = jnp.concatenate(
        [jnp.where(lane_head == h, kmean, 0.0) for h in range(N_HEADS_B)], axis=0)
    q_hi, q_lo = _split_bf16(qb)
    w_hi, w_lo = _split_bf16(wt)
    gate_ref[0] = _dot_nt(w_hi, q_hi) + (_dot_nt(w_lo, q_hi) + _dot_nt(w_hi, q_lo))


def _project(x, g_attn, w_in_bf16):
    b, s, d = x.shape
    n_blocks = s // MOBA_BLOCK
    slope_hi, slope_lo = _moba_slope_split()
    return pl.pallas_call(
        functools.partial(_proj_kernel, slope_hi=slope_hi, slope_lo=slope_lo),
        grid=(b, s // PROJ_ROWS),
        in_specs=[
            pl.BlockSpec((1, PROJ_ROWS, d), lambda i, t: (i, t, 0)),
            pl.BlockSpec((1, d), lambda i, t: (0, 0)),
            pl.BlockSpec((d, 3 * d), lambda i, t: (0, 0)),
        ],
        out_specs=[
            pl.BlockSpec((1, PROJ_ROWS, 3 * WIDTH_A), lambda i, t: (i, t, 0)),
            pl.BlockSpec((1, LANES, PROJ_ROWS), lambda i, t: (i, 0, t)),
            pl.BlockSpec((1, 4, PROJ_ROWS // 4, 3 * WIDTH_A), lambda i, t: (i, 0, t, 0)),
            pl.BlockSpec((1, 16, PROJ_ROWS // 16, 3 * WIDTH_A), lambda i, t: (i, 0, t, 0)),
            pl.BlockSpec((1, PROJ_ROWS, 6 * WIDTH_B), lambda i, t: (i, t, 0)),
        ],
        out_shape=[
            jax.ShapeDtypeStruct((b, s, 3 * WIDTH_A), jnp.bfloat16),
            jax.ShapeDtypeStruct((b, LANES, s), jnp.float32),
            jax.ShapeDtypeStruct((b, 4, s // 4, 3 * WIDTH_A), jnp.bfloat16),
            jax.ShapeDtypeStruct((b, 16, s // 16, 3 * WIDTH_A), jnp.bfloat16),
            jax.ShapeDtypeStruct((b, s, 6 * WIDTH_B), jnp.bfloat16),
        ],
        scratch_shapes=[pltpu.VMEM((n_blocks, WIDTH_B), jnp.float32),
                        pltpu.VMEM((4, PROJ_ROWS, LANES), jnp.float32)],
        compiler_params=pltpu.CompilerParams(
            dimension_semantics=("arbitrary", "arbitrary"), vmem_limit_bytes=VMEM_LIMIT),
        name="proj",
    )(x, g_attn, w_in_bf16)


def _select_kernel(gate_ref, sel_ref):
    n_blocks = LANES // N_HEADS_B
    width = gate_ref.shape[2]
    q_pos = pl.program_id(1) * width + lax.broadcasted_iota(jnp.int32, (n_blocks, width), 1)
    blk = lax.broadcasted_iota(jnp.int32, (n_blocks, width), 0)
    past = blk < q_pos // MOBA_BLOCK
    for h in range(N_HEADS_B):
        g = jnp.where(past, gate_ref[0, h * n_blocks:(h + 1) * n_blocks, :], NEG_INF)
        rank = jnp.zeros(g.shape, jnp.int32)
        for other_blk in range(n_blocks):
            other = g[other_blk:other_blk + 1, :]
            beats = (other > g) | ((other == g) & (other_blk < blk))
            rank = rank + beats.astype(jnp.int32)
        sel_ref[0, h * n_blocks:(h + 1) * n_blocks, :] = (past & (rank < MOBA_TOPK)).astype(sel_ref.dtype)


def _select(gate):
    b, _, s = gate.shape
    width = 1024
    return pl.pallas_call(
        _select_kernel,
        grid=(b, s // width),
        in_specs=[pl.BlockSpec((1, LANES, width), lambda i, t: (i, 0, t))],
        out_specs=pl.BlockSpec((1, LANES, width), lambda i, t: (i, 0, t)),
        out_shape=jax.ShapeDtypeStruct((b, LANES, s), jnp.float32),
        compiler_params=pltpu.CompilerParams(dimension_semantics=("arbitrary", "arbitrary")),
        name="select",
    )(gate)


def _dilated_kernel(slopes_ref, q1_ref, k1_ref, v1_ref, q4_ref, k4_ref, v4_ref,
                    q16_ref, k16_ref, v16_ref, o_ref,
                    acc_ref, stat_ref, bias_ref, s_ref, p_ref, mloc_ref, stats_ref):
    pair = pl.program_id(0)
    seq = k1_ref.shape[1]
    c = DIL_BLOCK
    n_steps = seq // c
    dilations = tuple(d for _, d in DILATED_PAIRS)

    @pl.when(pl.program_id(1) == 0)
    def _():
        key = lax.broadcasted_iota(jnp.int32, (2 * c, 2 * c), 0)
        lane = lax.broadcasted_iota(jnp.int32, (2 * c, 2 * c), 1)
        slope_l = jnp.where(lane >= c, slopes_ref[2 * pair + 1], slopes_ref[2 * pair])
        for b_idx, dil in enumerate(dilations):
            for first in range(2):
                dist = (0 if first else c) + jnp.bitwise_and(lane, c - 1) - key
                bias_ref[2 * b_idx + first] = jnp.where(
                    (dist >= 0) & (dist <= c), (-dil) * slope_l * dist.astype(jnp.float32), NEG_INF)

    head0 = lax.broadcasted_iota(jnp.int32, (c, LANES), 1) < HEAD_DIM

    branches = ((q1_ref, k1_ref, v1_ref), (q4_ref, k4_ref, v4_ref), (q16_ref, k16_ref, v16_ref))
    for b_idx, (dil, (q_ref, k_ref, v_ref)) in enumerate(zip(dilations, branches)):
        sub_len = seq // dil
        n_chunks = sub_len // c

        def offsets(t, n_chunks=n_chunks, sub_len=sub_len):
            t = jnp.asarray(t, jnp.int32)
            r = t // n_chunks
            n = t % n_chunks
            q_off = pl.multiple_of(r * sub_len + n * c, c)
            k_off = pl.multiple_of(r * sub_len + jnp.maximum(n - 1, 0) * c, c)
            return r, n, q_off, k_off

        def scores(t, slot, b_idx=b_idx, q_ref=q_ref, k_ref=k_ref, offsets=offsets):
            _, n, q_off, k_off = offsets(t)
            q = q_ref[0, pl.ds(q_off, c), :]
            q_heads = jnp.concatenate([jnp.where(head0, q, jnp.zeros_like(q)),
                                       jnp.where(head0, jnp.zeros_like(q), q)], axis=0)
            keys = k_ref[0, pl.ds(k_off, 2 * c), :]
            first = (n == 0).astype(jnp.int32)
            st = _dot_nt(keys, q_heads) + bias_ref[2 * b_idx + first]
            s_ref[slot] = st
            mloc_ref[slot, 0:1, :] = jnp.max(st, axis=0, keepdims=True)

        def softmax(t, slot):
            m_loc = mloc_ref[slot, 0:1, :]
            p = jnp.exp2(s_ref[slot] - m_loc)
            stats_ref[slot, 0:1, :] = m_loc
            stats_ref[slot, 1:2, :] = jnp.sum(p, axis=0, keepdims=True)
            p_ref[slot] = p.astype(p_ref.dtype)

        def emit(t, slot, b_idx=b_idx, dil=dil, v_ref=v_ref, offsets=offsets):
            r, n, _, k_off = offsets(t)
            vals = v_ref[0, pl.ds(k_off, 2 * c), :]
            pv = lax.dot_general(vals, p_ref[slot], (((0,), (0,)), ((), ())),
                                 preferred_element_type=jnp.float32)
            out_t = jnp.concatenate([pv[:HEAD_DIM, :c], pv[HEAD_DIM:, c:]], axis=0)
            if dil == 1:
                rows = pl.ds(pl.multiple_of(n * c, c), c)
            else:
                rows = pl.ds(r + dil * c * n, c, stride=dil)
            l_loc = stats_ref[slot, 1:2, :]
            inv_l = 1.0 / l_loc
            lse = stats_ref[slot, 0:1, :] + jnp.log2(l_loc)
            out_t = out_t * jnp.concatenate([jnp.broadcast_to(inv_l[:, :c], (HEAD_DIM, c)),
                                             jnp.broadcast_to(inv_l[:, c:], (HEAD_DIM, c))], axis=0)
            lse_t = jnp.concatenate([jnp.broadcast_to(lse[:, :c], (HEAD_DIM, c)),
                                     jnp.broadcast_to(lse[:, c:], (HEAD_DIM, c))], axis=0)
            acc_ref[b_idx, rows, :] = out_t.T
            stat_ref[b_idx, rows, :] = lse_t.T

        _software_pipeline(n_steps, scores, softmax, emit, DIL_LAG)

    def merge(ci, carry):
        rows = pl.ds(pl.multiple_of(ci * c, c), c)
        lses = [stat_ref[b_idx, rows, :] for b_idx in range(len(dilations))]
        lse_max = functools.reduce(jnp.maximum, lses)
        weights = [jnp.exp2(lse - lse_max) for lse in lses]
        out = functools.reduce(lambda a, b: a + b,
                               [w * acc_ref[b_idx, rows, :] for b_idx, w in enumerate(weights)])
        o_ref[0, rows, :] = (out / functools.reduce(lambda a, b: a + b, weights)).astype(o_ref.dtype)
        return carry

    lax.fori_loop(0, seq // c, merge, 0, unroll=4)


def _dilated(qkv_by_dilation, slopes):
    b, s, _ = qkv_by_dilation[0].shape
    n_pairs = WIDTH_A // LANES
    tok_major = pl.BlockSpec((1, s, LANES), lambda p, i: (i, 0, p))
    operands, in_specs = [slopes], [pl.BlockSpec(memory_space=pltpu.SMEM)]
    for arr in qkv_by_dilation:
        for part in range(3):
            operands.append(arr)
            in_specs.append(pl.BlockSpec((1, s, LANES), lambda p, i, part=part: (i, 0, part * n_pairs + p)))
    n_br = len(DILATED_PAIRS)
    c = DIL_BLOCK
    return pl.pallas_call(
        _dilated_kernel,
        grid=(n_pairs, b),
        in_specs=in_specs,
        out_specs=tok_major,
        out_shape=jax.ShapeDtypeStruct((b, s, WIDTH_A), jnp.float32),
        scratch_shapes=[pltpu.VMEM((n_br, s, LANES), jnp.float32),
                        pltpu.VMEM((n_br, s, LANES), jnp.float32),
                        pltpu.VMEM((2 * n_br, 2 * c, 2 * c), jnp.float32),
                        pltpu.VMEM((2 * DIL_LAG, 2 * c, 2 * c), jnp.float32),
                        pltpu.VMEM((2 * DIL_LAG, 2 * c, 2 * c), jnp.bfloat16),
                        pltpu.VMEM((2 * DIL_LAG, 8, 2 * c), jnp.float32),
                        pltpu.VMEM((2 * DIL_LAG, 8, 2 * c), jnp.float32)],
        compiler_params=pltpu.CompilerParams(
            dimension_semantics=("arbitrary", "arbitrary"), vmem_limit_bytes=VMEM_LIMIT),
        name="dilated",
    )(*operands)


M_INIT = -1e30


def _moba_pair_tables(n_blocks):
    pairs = [(i, j) for i in range(1, n_blocks) for j in range(i)]
    return np.array([p[0] for p in pairs], np.int32), np.array([p[1] for p in pairs], np.int32)


def _moba_kernel(slopes_ref, qblk_ref, kblk_ref, q0_ref, q1_ref, k0_ref, k1_ref, v0_ref, v1_ref, sel_ref,
                 o_ref, acc_ref, m_ref, l_ref, mask_ref, *rings):
    s_ref, p_ref, mloc_ref, stats_ref = (rings[n * MOBA_RING:(n + 1) * MOBA_RING] for n in range(4))
    pair = pl.program_id(0)
    bs = MOBA_BLOCK
    n_blocks = k0_ref.shape[1] // bs
    n_past = qblk_ref.shape[0]
    sel_rows = LANES // N_HEADS_B
    slopes = (slopes_ref[2 * pair], slopes_ref[2 * pair + 1])
    q_refs, k_refs, v_refs = (q0_ref, q1_ref), (k0_ref, k1_ref), (v0_ref, v1_ref)
    sum_row = (HEAD_DIM, 0)

    @pl.when(pl.program_id(1) == 0)
    def _():
        rel = (lax.broadcasted_iota(jnp.int32, (bs, bs), 0)
               - lax.broadcasted_iota(jnp.int32, (bs, bs), 1))
        mask_ref[...] = jnp.where(rel <= 0, 0.0, NEG_INF)

    m_ref[...] = jnp.full_like(m_ref, M_INIT)
    l_ref[...] = jnp.zeros_like(l_ref)
    acc_ref[...] = jnp.zeros_like(acc_ref)
    q_pos = lax.broadcasted_iota(jnp.int32, (1, bs), 1).astype(jnp.float32)

    def make_stages(blocks_of, own):
        def scores(t, slot):
            i, j = blocks_of(t)
            q_off = pl.multiple_of(i * bs, bs)
            k_off = pl.multiple_of(j * bs, bs)
            for h in range(2):
                st = _dot_nt(k_refs[h][0, pl.ds(k_off, bs), :], q_refs[h][0, pl.ds(q_off, bs), :])
                if own:
                    st = st + mask_ref[...]
                s_ref[slot][h] = st
                mloc_ref[slot][h:h + 1, :] = jnp.max(st, axis=0, keepdims=True)

        def softmax(t, slot):
            for h in range(2):
                m_loc = mloc_ref[slot][h:h + 1, :]
                stats_ref[slot][h:h + 1, :] = m_loc
                p_ref[slot][h] = jnp.exp2(s_ref[slot][h] - m_loc).astype(p_ref[slot].dtype)

        def merge(t, slot):
            i, j = blocks_of(t)
            q_off = pl.multiple_of(i * bs, bs)
            k_off = pl.multiple_of(j * bs, bs)
            gap = ((i - j) * bs).astype(jnp.float32)
            for h in range(2):
                pv = lax.dot_general(v_refs[h][0, pl.ds(k_off, bs), :], p_ref[slot][h],
                                     (((0,), (0,)), ((), ())), preferred_element_type=jnp.float32)
                out_t = pv[h * HEAD_DIM:(h + 1) * HEAD_DIM]
                l_blk = pv[sum_row[h]:sum_row[h] + 1]
                m_blk = stats_ref[slot][h:h + 1, :] - slopes[h] * (q_pos + gap)
                m_old = m_ref[h:h + 1, pl.ds(q_off, bs)]
                if own:
                    m_new = jnp.maximum(m_old, m_blk)
                    a_blk = jnp.exp2(m_blk - m_new)
                else:
                    chosen = sel_ref[0, pl.ds(h * sel_rows + j, 1), pl.ds(q_off, bs)] > 0.5
                    m_new = jnp.where(chosen, jnp.maximum(m_old, m_blk), m_old)
                    a_blk = jnp.where(chosen, jnp.exp2(m_blk - m_new), 0.0)
                a_old = jnp.exp2(m_old - m_new)
                acc_ref[h, :, pl.ds(q_off, bs)] = a_old * acc_ref[h, :, pl.ds(q_off, bs)] + a_blk * out_t
                l_ref[h:h + 1, pl.ds(q_off, bs)] = a_old * l_ref[h:h + 1, pl.ds(q_off, bs)] + a_blk * l_blk
                m_ref[h:h + 1, pl.ds(q_off, bs)] = m_new

        return scores, softmax, merge

    def own_blocks(t):
        i = jnp.asarray(t, jnp.int32)
        return i, i

    _software_pipeline(n_blocks, *make_stages(own_blocks, own=True), MOBA_OWN_LAG, scores_first=False)
    _software_pipeline(n_past, *make_stages(lambda t: (qblk_ref[t], kblk_ref[t]), own=False),
                       MOBA_LAG, scores_first=False)

    def finish(i, carry):
        q0 = pl.multiple_of(i * bs, bs)
        halves = [acc_ref[h, :, pl.ds(q0, bs)] / l_ref[h:h + 1, pl.ds(q0, bs)] for h in range(2)]
        o_ref[0, pl.ds(q0, bs), :] = jnp.concatenate(halves, axis=0).T.astype(o_ref.dtype)
        return carry

    lax.fori_loop(0, n_blocks, finish, 0, unroll=4)


def _moba(heads, sel, slopes):
    b, s, _ = heads.shape
    n_pairs = WIDTH_B // LANES
    sel_rows = 2 * (LANES // N_HEADS_B)
    qblk, kblk = _moba_pair_tables(s // MOBA_BLOCK)
    smem = pl.BlockSpec(memory_space=pltpu.SMEM)
    tile = lambda n: pl.BlockSpec((1, s, LANES), lambda p, i: (i, 0, 6 * p + n))
    return pl.pallas_call(
        _moba_kernel,
        grid=(n_pairs, b),
        in_specs=[smem, smem, smem] + [tile(n) for n in range(6)]
                 + [pl.BlockSpec((1, sel_rows, s), lambda p, i: (i, p, 0))],
        out_specs=pl.BlockSpec((1, s, LANES), lambda p, i: (i, 0, p)),
        out_shape=jax.ShapeDtypeStruct((b, s, WIDTH_B), jnp.float32),
        scratch_shapes=[pltpu.VMEM((2, HEAD_DIM, s), jnp.float32),
                        pltpu.VMEM((8, s), jnp.float32),
                        pltpu.VMEM((8, s), jnp.float32),
                        pltpu.VMEM((MOBA_BLOCK, MOBA_BLOCK), jnp.float32)]
                       + [pltpu.VMEM((2, MOBA_BLOCK, MOBA_BLOCK), jnp.float32)] * MOBA_RING
                       + [pltpu.VMEM((2, MOBA_BLOCK, MOBA_BLOCK), jnp.bfloat16)] * MOBA_RING
                       + [pltpu.VMEM((8, MOBA_BLOCK), jnp.float32)] * (2 * MOBA_RING),
        compiler_params=pltpu.CompilerParams(
            dimension_semantics=("arbitrary", "arbitrary"), vmem_limit_bytes=VMEM_LIMIT),
        name="moba",
    )(slopes, jnp.asarray(qblk), jnp.asarray(kblk), *([heads] * 6), sel)


def _rms(x, g):
    return (x * lax.rsqrt(jnp.mean(x * x, axis=-1, keepdims=True) + EPS)) * g


def _tail_kernel(x_ref, p_ref, oa_ref, ob_ref, ga_ref, gb_ref, wout_ref, gmlp_ref, wup_ref, wdown_ref,
                 gple_ref, wgate_ref, bgate_ref, wproj_ref, gfin_ref, o_ref):
    bf = jnp.bfloat16
    ya = _rms(oa_ref[0].astype(jnp.float32), ga_ref[...]).astype(bf)
    yb = _rms(ob_ref[0].astype(jnp.float32), gb_ref[...]).astype(bf)
    y = (jnp.dot(ya, wout_ref[0:WIDTH_A, :], preferred_element_type=jnp.float32)
         + jnp.dot(yb, wout_ref[WIDTH_A:, :], preferred_element_type=jnp.float32))
    h = x_ref[0] + y

    hn = _rms(h, gmlp_ref[...]).astype(bf)
    ff_chunk = 1024
    for c in range(D_FF // ff_chunk):
        u = jnp.dot(hn, wup_ref[:, c * ff_chunk:(c + 1) * ff_chunk], preferred_element_type=jnp.float32)
        u = jnp.square(jnp.maximum(u, 0.0)).astype(bf)
        h = h + jnp.dot(u, wdown_ref[c * ff_chunk:(c + 1) * ff_chunk, :], preferred_element_type=jnp.float32)

    z = jnp.dot(_rms(h, gple_ref[...]).astype(bf), wgate_ref[...], preferred_element_type=jnp.float32)
    gate = jax.nn.sigmoid(z + bgate_ref[...])
    h = h + gate * jnp.dot(p_ref[0].astype(bf), wproj_ref[...], preferred_element_type=jnp.float32)
    o_ref[0] = _rms(h, gfin_ref[...]).astype(o_ref.dtype)


def _tail(x, p, oa, ob, ga, gb, wout, gmlp, wup, wdown, gple, wgate, bgate, wproj, gfin):
    b, s, d = x.shape
    rows = lambda width: pl.BlockSpec((1, TAIL_ROWS, width), lambda i, t: (i, t, 0))
    full = lambda arr: pl.BlockSpec(arr.shape, lambda i, t: (0,) * arr.ndim,
                                    pipeline_mode=pl.Buffered(1))
    consts = (ga, gb, wout, gmlp, wup, wdown, gple, wgate, bgate, wproj, gfin)
    return pl.pallas_call(
        _tail_kernel,
        grid=(b, s // TAIL_ROWS),
        in_specs=[rows(d), rows(PLE_DIM), rows(WIDTH_A), rows(WIDTH_B)] + [full(a) for a in consts],
        out_specs=rows(d),
        out_shape=jax.ShapeDtypeStruct((b, s, d), x.dtype),
        compiler_params=pltpu.CompilerParams(
            dimension_semantics=("arbitrary", "arbitrary"), vmem_limit_bytes=VMEM_LIMIT),
        name="tail",
    )(x, p, oa, ob, *consts)


def kernel(x, p, g_attn, w_in, g_out_a, g_out_b, w_out, g_mlp, w_up, w_down, g_ple, w_ple_gate,
           b_ple_gate, w_ple_proj, g_final):
    assert x.shape[1:] == (4096, D_MODEL) and p.shape[0] == 1 and w_in.shape[0] == 1
    b, s, _ = x.shape
    bf = jnp.bfloat16
    slopes_a, slopes_b = _alibi_log2_slopes()
    row = lambda v: v.reshape(1, -1)

    qkv_a, gate, qkv_a4, qkv_a16, moba_heads = _project(x, row(g_attn[0]), w_in[0].astype(bf))
    sel = _select(gate)
    oa = _dilated([qkv_a, qkv_a4.reshape(b, s, 3 * WIDTH_A), qkv_a16.reshape(b, s, 3 * WIDTH_A)],
                  jnp.asarray(slopes_a))
    ob = _moba(moba_heads, sel, jnp.asarray(slopes_b))

    return _tail(x, p[0], oa, ob, row(g_out_a[0]), row(g_out_b[0]), w_out[0].astype(bf), row(g_mlp[0]),
                 w_up[0].astype(bf), w_down[0].astype(bf), row(g_ple[0]), w_ple_gate[0].astype(bf),
                 row(b_ple_gate[0]), w_ple_proj[0].astype(bf), row(g_final))
```

```python
import functools

import jax
import jax.numpy as jnp
import numpy as np
from jax import lax
from jax.experimental import pallas as pl
from jax.experimental.pallas import tpu as pltpu

D_MODEL = 1024
HEAD_DIM = 64
N_HEADS = D_MODEL // HEAD_DIM
N_HEADS_A = N_HEADS // 2
N_HEADS_B = N_HEADS - N_HEADS_A
WIDTH_A = N_HEADS_A * HEAD_DIM
WIDTH_B = N_HEADS_B * HEAD_DIM
DILATED_PAIRS = ((128, 1), (512, 4), (2048, 16))
MOBA_BLOCK = 256
MOBA_TOPK = 3
D_FF = 4 * D_MODEL
PLE_DIM = 256
EPS = 1e-6

LANES = 128
LOG2E = 1.4426950408889634
Q_SCALE = HEAD_DIM ** -0.5 * LOG2E
NEG_INF = float("-inf")
VMEM_V7X_BYTES = 64 * 1024 * 1024
VMEM_LIMIT = VMEM_V7X_BYTES * 7 // 8

PROJ_ROWS = 512
TAIL_ROWS = 512
DIL_BLOCK = 128


def _alibi_log2_slopes():
    s = np.array([2.0 ** (-8.0 * (i + 1) / N_HEADS) for i in range(N_HEADS)], dtype=np.float64)
    return (s[0::2] * LOG2E).astype(np.float32), (s[1::2] * LOG2E).astype(np.float32)


def _dot_nt(a, b):
    return lax.dot_general(a, b, (((1,), (1,)), ((), ())), preferred_element_type=jnp.float32)


def _split_bf16(x):
    hi = x.astype(jnp.bfloat16)
    lo = (x - hi.astype(jnp.float32)).astype(jnp.bfloat16)
    return hi, lo


DIL_LAG = 4
MOBA_OWN_LAG = 4
MOBA_LAG = 15
MOBA_RING = 2 * MOBA_LAG


def _software_pipeline(n_steps, scores, softmax, emit, lag, scores_first=True):
    ring = 2 * lag
    assert n_steps % ring == 0
    for t in range(ring):
        scores(t, t)
    for t in range(lag):
        softmax(t, t)

    def body(it, carry, last=False):
        for k in range(ring):
            t = ring * it + k
            if scores_first and not last:
                scores(t + ring, k)
            if not last or k < lag:
                softmax(t + lag, (k + lag) % ring)
            emit(t, k)
            if not scores_first and not last:
                scores(t + ring, k)
        return carry

    lax.fori_loop(0, n_steps // ring - 1, body, 0)
    body(n_steps // ring - 1, 0, last=True)


def _moba_slope_split():
    _, slopes = _alibi_log2_slopes()
    hi = slopes.astype(jnp.bfloat16).astype(np.float32)
    lo = (slopes - hi).astype(jnp.bfloat16).astype(np.float32)
    return hi, lo


def _proj_kernel(x_ref, g_ref, w_ref, qkv_ref, gate_ref, a4_ref, a16_ref, moba_ref, kmean_ref, slab_ref,
                 *, slope_hi, slope_lo):
    t = pl.program_id(1)
    x = x_ref[0]
    ms = jnp.mean(x * x, axis=-1, keepdims=True)
    hn = (x * lax.rsqrt(ms + EPS)) * g_ref[...]
    qkv = jnp.dot(hn.astype(jnp.bfloat16), w_ref[...], preferred_element_type=jnp.float32)

    qa = qkv[:, 0:WIDTH_A]
    qb = qkv[:, 3 * WIDTH_A:3 * WIDTH_A + WIDTH_B]
    kb = qkv[:, 3 * WIDTH_A + WIDTH_B:3 * WIDTH_A + 2 * WIDTH_B]
    vb = qkv[:, 3 * WIDTH_A + 2 * WIDTH_B:]
    qkv_ref[0, :, 0:WIDTH_A] = (qa * Q_SCALE).astype(qkv_ref.dtype)
    qkv_ref[0, :, WIDTH_A:] = qkv[:, WIDTH_A:3 * WIDTH_A].astype(qkv_ref.dtype)

    lane = lax.broadcasted_iota(jnp.int32, (PROJ_ROWS, LANES), 1)
    key_pos = jnp.bitwise_and(lax.broadcasted_iota(jnp.int32, (PROJ_ROWS, LANES), 0),
                              MOBA_BLOCK - 1).astype(jnp.float32)
    for pair in range(WIDTH_B // LANES):
        cols = slice(pair * LANES, (pair + 1) * LANES)
        q_t, k_t, v_t = qb[:, cols] * Q_SCALE, kb[:, cols], vb[:, cols]
        for h in range(2):
            own = (lane < HEAD_DIM) if h == 0 else (lane >= HEAD_DIM)
            f0 = HEAD_DIM if h == 0 else 0
            head = 2 * pair + h
            q_feat = jnp.where(lane == f0, float(slope_hi[head]),
                               jnp.where(lane == f0 + 1, float(slope_lo[head]), 0.0))
            k_feat = jnp.where((lane == f0) | (lane == f0 + 1), key_pos, 0.0)
            base = (pair * 6 + h) * LANES
            for n, tile in enumerate((jnp.where(own, q_t, q_feat), jnp.where(own, k_t, k_feat),
                                      jnp.where(own, v_t, 1.0))):
                moba_ref[0, :, base + 2 * n * LANES:base + (2 * n + 1) * LANES] = tile.astype(moba_ref.dtype)

    rows4 = PROJ_ROWS // 4
    rows16 = PROJ_ROWS // 16
    for cb in range(3 * WIDTH_A // LANES):
        cols = slice(cb * LANES, (cb + 1) * LANES)
        val = qkv[:, cols]
        nat, res4 = 2 * (cb % 2), 2 * (cb % 2) + 1
        slab_ref[nat] = val * Q_SCALE if cb < WIDTH_A // LANES else val
        for r4 in range(4):
            part = slab_ref[nat, pl.ds(r4, rows4, stride=4), :]
            a4_ref[0, r4, :, cols] = part.astype(a4_ref.dtype)
            slab_ref[res4, r4 * rows4:(r4 + 1) * rows4, :] = part
        for r4 in range(4):
            for j in range(4):
                part = slab_ref[res4, pl.ds(r4 * rows4 + j, rows16, stride=4), :]
                a16_ref[0, 4 * j + r4, :, cols] = part.astype(a16_ref.dtype)

    @pl.when(t == 0)
    def _():
        kmean_ref[...] = jnp.zeros_like(kmean_ref)

    blocks_per_tile = PROJ_ROWS // MOBA_BLOCK
    for sb in range(blocks_per_tile):
        km = jnp.mean(kb[sb * MOBA_BLOCK:(sb + 1) * MOBA_BLOCK, :], axis=0, keepdims=True)
        kmean_ref[pl.ds(t * blocks_per_tile + sb, 1), :] = km

    kmean = kmean_ref[...]
    lane_head = lax.broadcasted_iota(jnp.int32, kmean.shape, 1) // HEAD_DIM
    wt = jnp.concatenate(
        [jnp.where(lane_head == h, kmean, 0.0) for h in range(N_HEADS_B)], axis=0)
    q_hi, q_lo = _split_bf16(qb)
    w_hi, w_lo = _split_bf16(wt)
    gate_ref[0] = _dot_nt(w_hi, q_hi) + (_dot_nt(w_lo, q_hi) + _dot_nt(w_hi, q_lo))


def _project(x, g_attn, w_in_bf16):
    b, s, d = x.shape
    n_blocks = s // MOBA_BLOCK
    slope_hi, slope_lo = _moba_slope_split()
    return pl.pallas_call(
        functools.partial(_proj_kernel, slope_hi=slope_hi, slope_lo=slope_lo),
        grid=(b, s // PROJ_ROWS),
        in_specs=[
            pl.BlockSpec((1, PROJ_ROWS, d), lambda i, t: (i, t, 0)),
            pl.BlockSpec((1, d), lambda i, t: (0, 0)),
            pl.BlockSpec((d, 3 * d), lambda i, t: (0, 0)),
        ],
        out_specs=[
            pl.BlockSpec((1, PROJ_ROWS, 3 * WIDTH_A), lambda i, t: (i, t, 0)),
            pl.BlockSpec((1, LANES, PROJ_ROWS), lambda i, t: (i, 0, t)),
            pl.BlockSpec((1, 4, PROJ_ROWS // 4, 3 * WIDTH_A), lambda i, t: (i, 0, t, 0)),
            pl.BlockSpec((1, 16, PROJ_ROWS // 16, 3 * WIDTH_A), lambda i, t: (i, 0, t, 0)),
            pl.BlockSpec((1, PROJ_ROWS, 6 * WIDTH_B), lambda i, t: (i, t, 0)),
        ],
        out_shape=[
            jax.ShapeDtypeStruct((b, s, 3 * WIDTH_A), jnp.bfloat16),
            jax.ShapeDtypeStruct((b, LANES, s), jnp.float32),
            jax.ShapeDtypeStruct((b, 4, s // 4, 3 * WIDTH_A), jnp.bfloat16),
            jax.ShapeDtypeStruct((b, 16, s // 16, 3 * WIDTH_A), jnp.bfloat16),
            jax.ShapeDtypeStruct((b, s, 6 * WIDTH_B), jnp.bfloat16),
        ],
        scratch_shapes=[pltpu.VMEM((n_blocks, WIDTH_B), jnp.float32),
                        pltpu.VMEM((4, PROJ_ROWS, LANES), jnp.float32)],
        compiler_params=pltpu.CompilerParams(
            dimension_semantics=("arbitrary", "arbitrary"), vmem_limit_bytes=VMEM_LIMIT),
        name="proj",
    )(x, g_attn, w_in_bf16)


def _select_kernel(gate_ref, sel_ref):
    n_blocks = LANES // N_HEADS_B
    width = gate_ref.shape[2]
    q_pos = pl.program_id(1) * width + lax.broadcasted_iota(jnp.int32, (n_blocks, width), 1)
    blk = lax.broadcasted_iota(jnp.int32, (n_blocks, width), 0)
    past = blk < q_pos // MOBA_BLOCK
    for h in range(N_HEADS_B):
        g = jnp.where(past, gate_ref[0, h * n_blocks:(h + 1) * n_blocks, :], NEG_INF)
        rank = jnp.zeros(g.shape, jnp.int32)
        for other_blk in range(n_blocks):
            other = g[other_blk:other_blk + 1, :]
            beats = (other > g) | ((other == g) & (other_blk < blk))
            rank = rank + beats.astype(jnp.int32)
        sel_ref[0, h * n_blocks:(h + 1) * n_blocks, :] = (past & (rank < MOBA_TOPK)).astype(sel_ref.dtype)


def _select(gate):
    b, _, s = gate.shape
    width = 1024
    return pl.pallas_call(
        _select_kernel,
        grid=(b, s // width),
        in_specs=[pl.BlockSpec((1, LANES, width), lambda i, t: (i, 0, t))],
        out_specs=pl.BlockSpec((1, LANES, width), lambda i, t: (i, 0, t)),
        out_shape=jax.ShapeDtypeStruct((b, LANES, s), jnp.float32),
        compiler_params=pltpu.CompilerParams(dimension_semantics=("arbitrary", "arbitrary")),
        name="select",
    )(gate)


def _dilated_kernel(slopes_ref, q1_ref, k1_ref, v1_ref, q4_ref, k4_ref, v4_ref,
                    q16_ref, k16_ref, v16_ref, o_ref,
                    acc_ref, stat_ref, bias_ref, s_ref, p_ref, mloc_ref, stats_ref):
    pair = pl.program_id(0)
    seq = k1_ref.shape[1]
    c = DIL_BLOCK
    n_steps = seq // c
    dilations = tuple(d for _, d in DILATED_PAIRS)

    @pl.when(pl.program_id(1) == 0)
    def _():
        key = lax.broadcasted_iota(jnp.int32, (2 * c, 2 * c), 0)
        lane = lax.broadcasted_iota(jnp.int32, (2 * c, 2 * c), 1)
        slope_l = jnp.where(lane >= c, slopes_ref[2 * pair + 1], slopes_ref[2 * pair])
        for b_idx, dil in enumerate(dilations):
            for first in range(2):
                dist = (0 if first else c) + jnp.bitwise_and(lane, c - 1) - key
                bias_ref[2 * b_idx + first] = jnp.where(
                    (dist >= 0) & (dist <= c), (-dil) * slope_l * dist.astype(jnp.float32), NEG_INF)

    head0 = lax.broadcasted_iota(jnp.int32, (c, LANES), 1) < HEAD_DIM

    branches = ((q1_ref, k1_ref, v1_ref), (q4_ref, k4_ref, v4_ref), (q16_ref, k16_ref, v16_ref))
    for b_idx, (dil, (q_ref, k_ref, v_ref)) in enumerate(zip(dilations, branches)):
        sub_len = seq // dil
        n_chunks = sub_len // c

        def offsets(t, n_chunks=n_chunks, sub_len=sub_len):
            t = jnp.asarray(t, jnp.int32)
            r = t // n_chunks
            n = t % n_chunks
            q_off = pl.multiple_of(r * sub_len + n * c, c)
            k_off = pl.multiple_of(r * sub_len + jnp.maximum(n - 1, 0) * c, c)
            return r, n, q_off, k_off

        def scores(t, slot, b_idx=b_idx, q_ref=q_ref, k_ref=k_ref, offsets=offsets):
            _, n, q_off, k_off = offsets(t)
            q = q_ref[0, pl.ds(q_off, c), :]
            q_heads = jnp.concatenate([jnp.where(head0, q, jnp.zeros_like(q)),
                                       jnp.where(head0, jnp.zeros_like(q), q)], axis=0)
            keys = k_ref[0, pl.ds(k_off, 2 * c), :]
            first = (n == 0).astype(jnp.int32)
            st = _dot_nt(keys, q_heads) + bias_ref[2 * b_idx + first]
            s_ref[slot] = st
            mloc_ref[slot, 0:1, :] = jnp.max(st, axis=0, keepdims=True)

        def softmax(t, slot):
            m_loc = mloc_ref[slot, 0:1, :]
            p = jnp.exp2(s_ref[slot] - m_loc)
            stats_ref[slot, 0:1, :] = m_loc
            stats_ref[slot, 1:2, :] = jnp.sum(p, axis=0, keepdims=True)
            p_ref[slot] = p.astype(p_ref.dtype)

        def emit(t, slot, b_idx=b_idx, dil=dil, v_ref=v_ref, offsets=offsets):
            r, n, _, k_off = offsets(t)
            vals = v_ref[0, pl.ds(k_off, 2 * c), :]
            pv = lax.dot_general(vals, p_ref[slot], (((0,), (0,)), ((), ())),
                                 preferred_element_type=jnp.float32)
            out_t = jnp.concatenate([pv[:HEAD_DIM, :c], pv[HEAD_DIM:, c:]], axis=0)
            if dil == 1:
                rows = pl.ds(pl.multiple_of(n * c, c), c)
            else:
                rows = pl.ds(r + dil * c * n, c, stride=dil)
            l_loc = stats_ref[slot, 1:2, :]
            inv_l = 1.0 / l_loc
            lse = stats_ref[slot, 0:1, :] + jnp.log2(l_loc)
            out_t = out_t * jnp.concatenate([jnp.broadcast_to(inv_l[:, :c], (HEAD_DIM, c)),
                                             jnp.broadcast_to(inv_l[:, c:], (HEAD_DIM, c))], axis=0)
            lse_t = jnp.concatenate([jnp.broadcast_to(lse[:, :c], (HEAD_DIM, c)),
                                     jnp.broadcast_to(lse[:, c:], (HEAD_DIM, c))], axis=0)
            acc_ref[b_idx, rows, :] = out_t.T
            stat_ref[b_idx, rows, :] = lse_t.T

        _software_pipeline(n_steps, scores, softmax, emit, DIL_LAG)

    def merge(ci, carry):
        rows = pl.ds(pl.multiple_of(ci * c, c), c)
        lses = [stat_ref[b_idx, rows, :] for b_idx in range(len(dilations))]
        lse_max = functools.reduce(jnp.maximum, lses)
        weights = [jnp.exp2(lse - lse_max) for lse in lses]
        out = functools.reduce(lambda a, b: a + b,
                               [w * acc_ref[b_idx, rows, :] for b_idx, w in enumerate(weights)])
        o_ref[0, rows, :] = (out / functools.reduce(lambda a, b: a + b, weights)).astype(o_ref.dtype)
        return carry

    lax.fori_loop(0, seq // c, merge, 0, unroll=4)


def _dilated(qkv_by_dilation, slopes):
    b, s, _ = qkv_by_dilation[0].shape
    n_pairs = WIDTH_A // LANES
    tok_major = pl.BlockSpec((1, s, LANES), lambda p, i: (i, 0, p))
    operands, in_specs = [slopes], [pl.BlockSpec(memory_space=pltpu.SMEM)]
    for arr in qkv_by_dilation:
        for part in range(3):
            operands.append(arr)
            in_specs.append(pl.BlockSpec((1, s, LANES), lambda p, i, part=part: (i, 0, part * n_pairs + p)))
    n_br = len(DILATED_PAIRS)
    c = DIL_BLOCK
    return pl.pallas_call(
        _dilated_kernel,
        grid=(n_pairs, b),
        in_specs=in_specs,
        out_specs=tok_major,
        out_shape=jax.ShapeDtypeStruct((b, s, WIDTH_A), jnp.float32),
        scratch_shapes=[pltpu.VMEM((n_br, s, LANES), jnp.float32),
                        pltpu.VMEM((n_br, s, LANES), jnp.float32),
                        pltpu.VMEM((2 * n_br, 2 * c, 2 * c), jnp.float32),
                        pltpu.VMEM((2 * DIL_LAG, 2 * c, 2 * c), jnp.float32),
                        pltpu.VMEM((2 * DIL_LAG, 2 * c, 2 * c), jnp.bfloat16),
                        pltpu.VMEM((2 * DIL_LAG, 8, 2 * c), jnp.float32),
                        pltpu.VMEM((2 * DIL_LAG, 8, 2 * c), jnp.float32)],
        compiler_params=pltpu.CompilerParams(
            dimension_semantics=("arbitrary", "arbitrary"), vmem_limit_bytes=VMEM_LIMIT),
        name="dilated",
    )(*operands)


M_INIT = -1e30


def _moba_pair_tables(n_blocks):
    pairs = [(i, j) for i in range(1, n_blocks) for j in range(i)]
    return np.array([p[0] for p in pairs], np.int32), np.array([p[1] for p in pairs], np.int32)


def _moba_kernel(slopes_ref, qblk_ref, kblk_ref, q0_ref, q1_ref, k0_ref, k1_ref, v0_ref, v1_ref, sel_ref,
                 o_ref, acc_ref, m_ref, l_ref, mask_ref, *rings):
    s_ref, p_ref, mloc_ref, stats_ref = (rings[n * MOBA_RING:(n + 1) * MOBA_RING] for n in range(4))
    pair = pl.program_id(0)
    bs = MOBA_BLOCK
    n_blocks = k0_ref.shape[1] // bs
    n_past = qblk_ref.shape[0]
    sel_rows = LANES // N_HEADS_B
    slopes = (slopes_ref[2 * pair], slopes_ref[2 * pair + 1])
    q_refs, k_refs, v_refs = (q0_ref, q1_ref), (k0_ref, k1_ref), (v0_ref, v1_ref)
    sum_row = (HEAD_DIM, 0)

    @pl.when(pl.program_id(1) == 0)
    def _():
        rel = (lax.broadcasted_iota(jnp.int32, (bs, bs), 0)
               - lax.broadcasted_iota(jnp.int32, (bs, bs), 1))
        mask_ref[...] = jnp.where(rel <= 0, 0.0, NEG_INF)

    m_ref[...] = jnp.full_like(m_ref, M_INIT)
    l_ref[...] = jnp.zeros_like(l_ref)
    acc_ref[...] = jnp.zeros_like(acc_ref)
    q_pos = lax.broadcasted_iota(jnp.int32, (1, bs), 1).astype(jnp.float32)

    def make_stages(blocks_of, own):
        def scores(t, slot):
            i, j = blocks_of(t)
            q_off = pl.multiple_of(i * bs, bs)
            k_off = pl.multiple_of(j * bs, bs)
            for h in range(2):
                st = _dot_nt(k_refs[h][0, pl.ds(k_off, bs), :], q_refs[h][0, pl.ds(q_off, bs), :])
                if own:
                    st = st + mask_ref[...]
                s_ref[slot][h] = st
                mloc_ref[slot][h:h + 1, :] = jnp.max(st, axis=0, keepdims=True)

        def softmax(t, slot):
            for h in range(2):
                m_loc = mloc_ref[slot][h:h + 1, :]
                stats_ref[slot][h:h + 1, :] = m_loc
                p_ref[slot][h] = jnp.exp2(s_ref[slot][h] - m_loc).astype(p_ref[slot].dtype)

        def merge(t, slot):
            i, j = blocks_of(t)
            q_off = pl.multiple_of(i * bs, bs)
            k_off = pl.multiple_of(j * bs, bs)
            gap = ((i - j) * bs).astype(jnp.float32)
            for h in range(2):
                pv = lax.dot_general(v_refs[h][0, pl.ds(k_off, bs), :], p_ref[slot][h],
                                     (((0,), (0,)), ((), ())), preferred_element_type=jnp.float32)
                out_t = pv[h * HEAD_DIM:(h + 1) * HEAD_DIM]
                l_blk = pv[sum_row[h]:sum_row[h] + 1]
                m_blk = stats_ref[slot][h:h + 1, :] - slopes[h] * (q_pos + gap)
                m_old = m_ref[h:h + 1, pl.ds(q_off, bs)]
                if own:
                    m_new = jnp.maximum(m_old, m_blk)
                    a_blk = jnp.exp2(m_blk - m_new)
                else:
                    chosen = sel_ref[0, pl.ds(h * sel_rows + j, 1), pl.ds(q_off, bs)] > 0.5
                    m_new = jnp.where(chosen, jnp.maximum(m_old, m_blk), m_old)
                    a_blk = jnp.where(chosen, jnp.exp2(m_blk - m_new), 0.0)
                a_old = jnp.exp2(m_old - m_new)
                acc_ref[h, :, pl.ds(q_off, bs)] = a_old * acc_ref[h, :, pl.ds(q_off, bs)] + a_blk * out_t
                l_ref[h:h + 1, pl.ds(q_off, bs)] = a_old * l_ref[h:h + 1, pl.ds(q_off, bs)] + a_blk * l_blk
                m_ref[h:h + 1, pl.ds(q_off, bs)] = m_new

        return scores, softmax, merge

    def own_blocks(t):
        i = jnp.asarray(t, jnp.int32)
        return i, i

    _software_pipeline(n_blocks, *make_stages(own_blocks, own=True), MOBA_OWN_LAG, scores_first=False)
    _software_pipeline(n_past, *make_stages(lambda t: (qblk_ref[t], kblk_ref[t]), own=False),
                       MOBA_LAG, scores_first=False)

    def finish(i, carry):
        q0 = pl.multiple_of(i * bs, bs)
        halves = [acc_ref[h, :, pl.ds(q0, bs)] / l_ref[h:h + 1, pl.ds(q0, bs)] for h in range(2)]
        o_ref[0, pl.ds(q0, bs), :] = jnp.concatenate(halves, axis=0).T.astype(o_ref.dtype)
        return carry

    lax.fori_loop(0, n_blocks, finish, 0, unroll=4)


def _moba(heads, sel, slopes):
    b, s, _ = heads.shape
    n_pairs = WIDTH_B // LANES
    sel_rows = 2 * (LANES // N_HEADS_B)
    qblk, kblk = _moba_pair_tables(s // MOBA_BLOCK)
    smem = pl.BlockSpec(memory_space=pltpu.SMEM)
    tile = lambda n: pl.BlockSpec((1, s, LANES), lambda p, i: (i, 0, 6 * p + n))
    return pl.pallas_call(
        _moba_kernel,
        grid=(n_pairs, b),
        in_specs=[smem, smem, smem] + [tile(n) for n in range(6)]
                 + [pl.BlockSpec((1, sel_rows, s), lambda p, i: (i, p, 0))],
        out_specs=pl.BlockSpec((1, s, LANES), lambda p, i: (i, 0, p)),
        out_shape=jax.ShapeDtypeStruct((b, s, WIDTH_B), jnp.float32),
        scratch_shapes=[pltpu.VMEM((2, HEAD_DIM, s), jnp.float32),
                        pltpu.VMEM((8, s), jnp.float32),
                        pltpu.VMEM((8, s), jnp.float32),
                        pltpu.VMEM((MOBA_BLOCK, MOBA_BLOCK), jnp.float32)]
                       + [pltpu.VMEM((2, MOBA_BLOCK, MOBA_BLOCK), jnp.float32)] * MOBA_RING
                       + [pltpu.VMEM((2, MOBA_BLOCK, MOBA_BLOCK), jnp.bfloat16)] * MOBA_RING
                       + [pltpu.VMEM((8, MOBA_BLOCK), jnp.float32)] * (2 * MOBA_RING),
        compiler_params=pltpu.CompilerParams(
            dimension_semantics=("arbitrary", "arbitrary"), vmem_limit_bytes=VMEM_LIMIT),
        name="moba",
    )(slopes, jnp.asarray(qblk), jnp.asarray(kblk), *([heads] * 6), sel)


def _rms(x, g):
    return (x * lax.rsqrt(jnp.mean(x * x, axis=-1, keepdims=True) + EPS)) * g


def _tail_kernel(x_ref, p_ref, oa_ref, ob_ref, ga_ref, gb_ref, wout_ref, gmlp_ref, wup_ref, wdown_ref,
                 gple_ref, wgate_ref, bgate_ref, wproj_ref, gfin_ref, o_ref):
    bf = jnp.bfloat16
    ya = _rms(oa_ref[0].astype(jnp.float32), ga_ref[...]).astype(bf)
    yb = _rms(ob_ref[0].astype(jnp.float32), gb_ref[...]).astype(bf)
    y = (jnp.dot(ya, wout_ref[0:WIDTH_A, :], preferred_element_type=jnp.float32)
         + jnp.dot(yb, wout_ref[WIDTH_A:, :], preferred_element_type=jnp.float32))
    h = x_ref[0] + y

    hn = _rms(h, gmlp_ref[...]).astype(bf)
    ff_chunk = 1024
    for c in range(D_FF // ff_chunk):
        u = jnp.dot(hn, wup_ref[:, c * ff_chunk:(c + 1) * ff_chunk], preferred_element_type=jnp.float32)
        u = jnp.square(jnp.maximum(u, 0.0)).astype(bf)
        h = h + jnp.dot(u, wdown_ref[c * ff_chunk:(c + 1) * ff_chunk, :], preferred_element_type=jnp.float32)

    z = jnp.dot(_rms(h, gple_ref[...]).astype(bf), wgate_ref[...], preferred_element_type=jnp.float32)
    gate = jax.nn.sigmoid(z + bgate_ref[...])
    h = h + gate * jnp.dot(p_ref[0].astype(bf), wproj_ref[...], preferred_element_type=jnp.float32)
    o_ref[0] = _rms(h, gfin_ref[...]).astype(o_ref.dtype)


def _tail(x, p, oa, ob, ga, gb, wout, gmlp, wup, wdown, gple, wgate, bgate, wproj, gfin):
    b, s, d = x.shape
    rows = lambda width: pl.BlockSpec((1, TAIL_ROWS, width), lambda i, t: (i, t, 0))
    full = lambda arr: pl.BlockSpec(arr.shape, lambda i, t: (0,) * arr.ndim,
                                    pipeline_mode=pl.Buffered(1))
    consts = (ga, gb, wout, gmlp, wup, wdown, gple, wgate, bgate, wproj, gfin)
    return pl.pallas_call(
        _tail_kernel,
        grid=(b, s // TAIL_ROWS),
        in_specs=[rows(d), rows(PLE_DIM), rows(WIDTH_A), rows(WIDTH_B)] + [full(a) for a in consts],
        out_specs=rows(d),
        out_shape=jax.ShapeDtypeStruct((b, s, d), x.dtype),
        compiler_params=pltpu.CompilerParams(
            dimension_semantics=("arbitrary", "arbitrary"), vmem_limit_bytes=VMEM_LIMIT),
        name="tail",
    )(x, p, oa, ob, *consts)


def kernel(x, p, g_attn, w_in, g_out_a, g_out_b, w_out, g_mlp, w_up, w_down, g_ple, w_ple_gate,
           b_ple_gate, w_ple_proj, g_final):
    assert x.shape[1:] == (4096, D_MODEL) and p.shape[0] == 1 and w_in.shape[0] == 1
    b, s, _ = x.shape
    bf = jnp.bfloat16
    slopes_a, slopes_b = _alibi_log2_slopes()
    row = lambda v: v.reshape(1, -1)

    qkv_a, gate, qkv_a4, qkv_a16, moba_heads = _project(x, row(g_attn[0]), w_in[0].astype(bf))
    sel = _select(gate)
    oa = _dilated([qkv_a, qkv_a4.reshape(b, s, 3 * WIDTH_A), qkv_a16.reshape(b, s, 3 * WIDTH_A)],
                  jnp.asarray(slopes_a))
    ob = _moba(moba_heads, sel, jnp.asarray(slopes_b))

    return _tail(x, p[0], oa, ob, row(g_out_a[0]), row(g_out_b[0]), w_out[0].astype(bf), row(g_mlp[0]),
                 w_up[0].astype(bf), w_down[0].astype(bf), row(g_ple[0]), w_ple_gate[0].astype(bf),
                 row(b_ple_gate[0]), w_ple_proj[0].astype(bf), row(g_final))
```

```python
import functools

import jax
import jax.numpy as jnp
import numpy as np
from jax import lax
from jax.experimental import pallas as pl
from jax.experimental.pallas import tpu as pltpu

D_MODEL = 1024
HEAD_DIM = 64
N_HEADS = D_MODEL // HEAD_DIM
N_HEADS_A = N_HEADS // 2
N_HEADS_B = N_HEADS - N_HEADS_A
WIDTH_A = N_HEADS_A * HEAD_DIM
WIDTH_B = N_HEADS_B * HEAD_DIM
DILATED_PAIRS = ((128, 1), (512, 4), (2048, 16))
MOBA_BLOCK = 256
MOBA_TOPK = 3
D_FF = 4 * D_MODEL
PLE_DIM = 256
EPS = 1e-6

LANES = 128
LOG2E = 1.4426950408889634
Q_SCALE = HEAD_DIM ** -0.5 * LOG2E
NEG_INF = float("-inf")
VMEM_V7X_BYTES = 64 * 1024 * 1024
VMEM_LIMIT = VMEM_V7X_BYTES * 7 // 8

PROJ_ROWS = 512
TAIL_ROWS = 512
DIL_BLOCK = 128


def _alibi_log2_slopes():
    s = np.array([2.0 ** (-8.0 * (i + 1) / N_HEADS) for i in range(N_HEADS)], dtype=np.float64)
    return (s[0::2] * LOG2E).astype(np.float32), (s[1::2] * LOG2E).astype(np.float32)


def _dot_nt(a, b):
    return lax.dot_general(a, b, (((1,), (1,)), ((), ())), preferred_element_type=jnp.float32)


def _split_bf16(x):
    hi = x.astype(jnp.bfloat16)
    lo = (x - hi.astype(jnp.float32)).astype(jnp.bfloat16)
    return hi, lo


DIL_LAG = 4
MOBA_OWN_LAG = 4
MOBA_LAG = 15
MOBA_RING = 2 * MOBA_LAG


def _software_pipeline(n_steps, scores, softmax, emit, lag, scores_first=True):
    ring = 2 * lag
    assert n_steps % ring == 0
    for t in range(ring):
        scores(t, t)
    for t in range(lag):
        softmax(t, t)

    def body(it, carry, last=False):
        for k in range(ring):
            t = ring * it + k
            if scores_first and not last:
                scores(t + ring, k)
            if not last or k < lag:
                softmax(t + lag, (k + lag) % ring)
            emit(t, k)
            if not scores_first and not last:
                scores(t + ring, k)
        return carry

    lax.fori_loop(0, n_steps // ring - 1, body, 0)
    body(n_steps // ring - 1, 0, last=True)


def _moba_slope_split():
    _, slopes = _alibi_log2_slopes()
    hi = slopes.astype(jnp.bfloat16).astype(np.float32)
    lo = (slopes - hi).astype(jnp.bfloat16).astype(np.float32)
    return hi, lo


def _proj_kernel(x_ref, g_ref, w_ref, qkv_ref, gate_ref, a4_ref, a16_ref, moba_ref, kmean_ref, slab_ref,
                 *, slope_hi, slope_lo):
    t = pl.program_id(1)
    x = x_ref[0]
    ms = jnp.mean(x * x, axis=-1, keepdims=True)
    hn = (x * lax.rsqrt(ms + EPS)) * g_ref[...]
    qkv = jnp.dot(hn.astype(jnp.bfloat16), w_ref[...], preferred_element_type=jnp.float32)

    qa = qkv[:, 0:WIDTH_A]
    qb = qkv[:, 3 * WIDTH_A:3 * WIDTH_A + WIDTH_B]
    kb = qkv[:, 3 * WIDTH_A + WIDTH_B:3 * WIDTH_A + 2 * WIDTH_B]
    vb = qkv[:, 3 * WIDTH_A + 2 * WIDTH_B:]
    qkv_ref[0, :, 0:WIDTH_A] = (qa * Q_SCALE).astype(qkv_ref.dtype)
    qkv_ref[0, :, WIDTH_A:] = qkv[:, WIDTH_A:3 * WIDTH_A].astype(qkv_ref.dtype)

    lane = lax.broadcasted_iota(jnp.int32, (PROJ_ROWS, LANES), 1)
    key_pos = jnp.bitwise_and(lax.broadcasted_iota(jnp.int32, (PROJ_ROWS, LANES), 0),
                              MOBA_BLOCK - 1).astype(jnp.float32)
    for pair in range(WIDTH_B // LANES):
        cols = slice(pair * LANES, (pair + 1) * LANES)
        q_t, k_t, v_t = qb[:, cols] * Q_SCALE, kb[:, cols], vb[:, cols]
        for h in range(2):
            own = (lane < HEAD_DIM) if h == 0 else (lane >= HEAD_DIM)
            f0 = HEAD_DIM if h == 0 else 0
            head = 2 * pair + h
            q_feat = jnp.where(lane == f0, float(slope_hi[head]),
                               jnp.where(lane == f0 + 1, float(slope_lo[head]), 0.0))
            k_feat = jnp.where((lane == f0) | (lane == f0 + 1), key_pos, 0.0)
            base = (pair * 6 + h) * LANES
            for n, tile in enumerate((jnp.where(own, q_t, q_feat), jnp.where(own, k_t, k_feat),
                                      jnp.where(own, v_t, 1.0))):
                moba_ref[0, :, base + 2 * n * LANES:base + (2 * n + 1) * LANES] = tile.astype(moba_ref.dtype)

    rows4 = PROJ_ROWS // 4
    rows16 = PROJ_ROWS // 16
    for cb in range(3 * WIDTH_A // LANES):
        cols = slice(cb * LANES, (cb + 1) * LANES)
        val = qkv[:, cols]
        nat, res4 = 2 * (cb % 2), 2 * (cb % 2) + 1
        slab_ref[nat] = val * Q_SCALE if cb < WIDTH_A // LANES else val
        for r4 in range(4):
            part = slab_ref[nat, pl.ds(r4, rows4, stride=4), :]
            a4_ref[0, r4, :, cols] = part.astype(a4_ref.dtype)
            slab_ref[res4, r4 * rows4:(r4 + 1) * rows4, :] = part
        for r4 in range(4):
            for j in range(4):
                part = slab_ref[res4, pl.ds(r4 * rows4 + j, rows16, stride=4), :]
                a16_ref[0, 4 * j + r4, :, cols] = part.astype(a16_ref.dtype)

    @pl.when(t == 0)
    def _():
        kmean_ref[...] = jnp.zeros_like(kmean_ref)

    blocks_per_tile = PROJ_ROWS // MOBA_BLOCK
    for sb in range(blocks_per_tile):
        km = jnp.mean(kb[sb * MOBA_BLOCK:(sb + 1) * MOBA_BLOCK, :], axis=0, keepdims=True)
        kmean_ref[pl.ds(t * blocks_per_tile + sb, 1), :] = km

    kmean = kmean_ref[...]
    lane_head = lax.broadcasted_iota(jnp.int32, kmean.shape, 1) // HEAD_DIM
    wt = jnp.concatenate(
        [jnp.where(lane_head == h, kmean, 0.0) for h in range(N_HEADS_B)], axis=0)
    q_hi, q_lo = _split_bf16(qb)
    w_hi, w_lo = _split_bf16(wt)
    gate_ref[0] = _dot_nt(w_hi, q_hi) + (_dot_nt(w_lo, q_hi) + _dot_nt(w_hi, q_lo))


def _project(x, g_attn, w_in_bf16):
    b, s, d = x.shape
    n_blocks = s // MOBA_BLOCK
    slope_hi, slope_lo = _moba_slope_split()
    return pl.pallas_call(
        functools.partial(_proj_kernel, slope_hi=slope_hi, slope_lo=slope_lo),
        grid=(b, s // PROJ_ROWS),
        in_specs=[
            pl.BlockSpec((1, PROJ_ROWS, d), lambda i, t: (i, t, 0)),
            pl.BlockSpec((1, d), lambda i, t: (0, 0)),
            pl.BlockSpec((d, 3 * d), lambda i, t: (0, 0)),
        ],
        out_specs=[
            pl.BlockSpec((1, PROJ_ROWS, 3 * WIDTH_A), lambda i, t: (i, t, 0)),
            pl.BlockSpec((1, LANES, PROJ_ROWS), lambda i, t: (i, 0, t)),
            pl.BlockSpec((1, 4, PROJ_ROWS // 4, 3 * WIDTH_A), lambda i, t: (i, 0, t, 0)),
            pl.BlockSpec((1, 16, PROJ_ROWS // 16, 3 * WIDTH_A), lambda i, t: (i, 0, t, 0)),
            pl.BlockSpec((1, PROJ_ROWS, 6 * WIDTH_B), lambda i, t: (i, t, 0)),
        ],
        out_shape=[
            jax.ShapeDtypeStruct((b, s, 3 * WIDTH_A), jnp.bfloat16),
            jax.ShapeDtypeStruct((b, LANES, s), jnp.float32),
            jax.ShapeDtypeStruct((b, 4, s // 4, 3 * WIDTH_A), jnp.bfloat16),
            jax.ShapeDtypeStruct((b, 16, s // 16, 3 * WIDTH_A), jnp.bfloat16),
            jax.ShapeDtypeStruct((b, s, 6 * WIDTH_B), jnp.bfloat16),
        ],
        scratch_shapes=[pltpu.VMEM((n_blocks, WIDTH_B), jnp.float32),
                        pltpu.VMEM((4, PROJ_ROWS, LANES), jnp.float32)],
        compiler_params=pltpu.CompilerParams(
            dimension_semantics=("arbitrary", "arbitrary"), vmem_limit_bytes=VMEM_LIMIT),
        name="proj",
    )(x, g_attn, w_in_bf16)


def _select_kernel(gate_ref, sel_ref):
    n_blocks = LANES // N_HEADS_B
    width = gate_ref.shape[2]
    q_pos = pl.program_id(1) * width + lax.broadcasted_iota(jnp.int32, (n_blocks, width), 1)
    blk = lax.broadcasted_iota(jnp.int32, (n_blocks, width), 0)
    past = blk < q_pos // MOBA_BLOCK
    for h in range(N_HEADS_B):
        g = jnp.where(past, gate_ref[0, h * n_blocks:(h + 1) * n_blocks, :], NEG_INF)
        picked = jnp.zeros(g.shape, jnp.bool_)
        for _ in range(MOBA_TOPK):
            best = jnp.max(g, axis=0, keepdims=True)
            first = jnp.min(jnp.where(g == best, blk, n_blocks), axis=0, keepdims=True)
            hit = blk == first
            picked = picked | hit
            g = jnp.where(hit, NEG_INF, g)
        sel_ref[0, h * n_blocks:(h + 1) * n_blocks, :] = (past & picked).astype(sel_ref.dtype)


def _select(gate):
    b, _, s = gate.shape
    width = 2048
    return pl.pallas_call(
        _select_kernel,
        grid=(b, s // width),
        in_specs=[pl.BlockSpec((1, LANES, width), lambda i, t: (i, 0, t))],
        out_specs=pl.BlockSpec((1, LANES, width), lambda i, t: (i, 0, t)),
        out_shape=jax.ShapeDtypeStruct((b, LANES, s), jnp.float32),
        compiler_params=pltpu.CompilerParams(dimension_semantics=("arbitrary", "arbitrary")),
        name="select",
    )(gate)


def _dilated_kernel(slopes_ref, q1_ref, k1_ref, v1_ref, q4_ref, k4_ref, v4_ref,
                    q16_ref, k16_ref, v16_ref, o_ref,
                    acc_ref, stat_ref, bias_ref, s_ref, p_ref, mloc_ref, stats_ref):
    pair = pl.program_id(0)
    seq = k1_ref.shape[1]
    c = DIL_BLOCK
    n_steps = seq // c
    dilations = tuple(d for _, d in DILATED_PAIRS)

    @pl.when(pl.program_id(1) == 0)
    def _():
        key = lax.broadcasted_iota(jnp.int32, (2 * c, 2 * c), 0)
        lane = lax.broadcasted_iota(jnp.int32, (2 * c, 2 * c), 1)
        slope_l = jnp.where(lane >= c, slopes_ref[2 * pair + 1], slopes_ref[2 * pair])
        for b_idx, dil in enumerate(dilations):
            for first in range(2):
                dist = (0 if first else c) + jnp.bitwise_and(lane, c - 1) - key
                bias_ref[2 * b_idx + first] = jnp.where(
                    (dist >= 0) & (dist <= c), (-dil) * slope_l * dist.astype(jnp.float32), NEG_INF)

    head0 = lax.broadcasted_iota(jnp.int32, (c, LANES), 1) < HEAD_DIM

    branches = ((q1_ref, k1_ref, v1_ref), (q4_ref, k4_ref, v4_ref), (q16_ref, k16_ref, v16_ref))
    for b_idx, (dil, (q_ref, k_ref, v_ref)) in enumerate(zip(dilations, branches)):
        sub_len = seq // dil
        n_chunks = sub_len // c

        def offsets(t, n_chunks=n_chunks, sub_len=sub_len):
            t = jnp.asarray(t, jnp.int32)
            r = t // n_chunks
            n = t % n_chunks
            q_off = pl.multiple_of(r * sub_len + n * c, c)
            k_off = pl.multiple_of(r * sub_len + jnp.maximum(n - 1, 0) * c, c)
            return r, n, q_off, k_off

        def scores(t, slot, b_idx=b_idx, q_ref=q_ref, k_ref=k_ref, offsets=offsets):
            _, n, q_off, k_off = offsets(t)
            q = q_ref[0, pl.ds(q_off, c), :]
            q_heads = jnp.concatenate([jnp.where(head0, q, jnp.zeros_like(q)),
                                       jnp.where(head0, jnp.zeros_like(q), q)], axis=0)
            keys = k_ref[0, pl.ds(k_off, 2 * c), :]
            first = (n == 0).astype(jnp.int32)
            st = _dot_nt(keys, q_heads) + bias_ref[2 * b_idx + first]
            s_ref[slot] = st
            mloc_ref[slot, 0:1, :] = jnp.max(st, axis=0, keepdims=True)

        def softmax(t, slot):
            m_loc = mloc_ref[slot, 0:1, :]
            p = jnp.exp2(s_ref[slot] - m_loc)
            stats_ref[slot, 0:1, :] = m_loc
            stats_ref[slot, 1:2, :] = jnp.sum(p, axis=0, keepdims=True)
            p_ref[slot] = p.astype(p_ref.dtype)

        def emit(t, slot, b_idx=b_idx, dil=dil, v_ref=v_ref, offsets=offsets):
            r, n, _, k_off = offsets(t)
            vals = v_ref[0, pl.ds(k_off, 2 * c), :]
            pv = lax.dot_general(vals, p_ref[slot], (((0,), (0,)), ((), ())),
                                 preferred_element_type=jnp.float32)
            out_t = jnp.concatenate([pv[:HEAD_DIM, :c], pv[HEAD_DIM:, c:]], axis=0)
            if dil == 1:
                rows = pl.ds(pl.multiple_of(n * c, c), c)
            else:
                rows = pl.ds(r + dil * c * n, c, stride=dil)
            l_loc = stats_ref[slot, 1:2, :]
            inv_l = 1.0 / l_loc
            lse = stats_ref[slot, 0:1, :] + jnp.log2(l_loc)
            out_t = out_t * jnp.concatenate([jnp.broadcast_to(inv_l[:, :c], (HEAD_DIM, c)),
                                             jnp.broadcast_to(inv_l[:, c:], (HEAD_DIM, c))], axis=0)
            lse_t = jnp.concatenate([jnp.broadcast_to(lse[:, :c], (HEAD_DIM, c)),
                                     jnp.broadcast_to(lse[:, c:], (HEAD_DIM, c))], axis=0)
            acc_ref[b_idx, rows, :] = out_t.T
            stat_ref[b_idx, rows, :] = lse_t.T

        _software_pipeline(n_steps, scores, softmax, emit, DIL_LAG)

    def merge(ci, carry):
        rows = pl.ds(pl.multiple_of(ci * c, c), c)
        lses = [stat_ref[b_idx, rows, :] for b_idx in range(len(dilations))]
        lse_max = functools.reduce(jnp.maximum, lses)
        weights = [jnp.exp2(lse - lse_max) for lse in lses]
        out = functools.reduce(lambda a, b: a + b,
                               [w * acc_ref[b_idx, rows, :] for b_idx, w in enumerate(weights)])
        o_ref[0, rows, :] = (out / functools.reduce(lambda a, b: a + b, weights)).astype(o_ref.dtype)
        return carry

    lax.fori_loop(0, seq // c, merge, 0, unroll=4)


def _dilated(qkv_by_dilation, slopes):
    b, s, _ = qkv_by_dilation[0].shape
    n_pairs = WIDTH_A // LANES
    tok_major = pl.BlockSpec((1, s, LANES), lambda p, i: (i, 0, p))
    operands, in_specs = [slopes], [pl.BlockSpec(memory_space=pltpu.SMEM)]
    for arr in qkv_by_dilation:
        for part in range(3):
            operands.append(arr)
            in_specs.append(pl.BlockSpec((1, s, LANES), lambda p, i, part=part: (i, 0, part * n_pairs + p)))
    n_br = len(DILATED_PAIRS)
    c = DIL_BLOCK
    return pl.pallas_call(
        _dilated_kernel,
        grid=(n_pairs, b),
        in_specs=in_specs,
        out_specs=tok_major,
        out_shape=jax.ShapeDtypeStruct((b, s, WIDTH_A), jnp.float32),
        scratch_shapes=[pltpu.VMEM((n_br, s, LANES), jnp.float32),
                        pltpu.VMEM((n_br, s, LANES), jnp.float32),
                        pltpu.VMEM((2 * n_br, 2 * c, 2 * c), jnp.float32),
                        pltpu.VMEM((2 * DIL_LAG, 2 * c, 2 * c), jnp.float32),
                        pltpu.VMEM((2 * DIL_LAG, 2 * c, 2 * c), jnp.bfloat16),
                        pltpu.VMEM((2 * DIL_LAG, 8, 2 * c), jnp.float32),
                        pltpu.VMEM((2 * DIL_LAG, 8, 2 * c), jnp.float32)],
        compiler_params=pltpu.CompilerParams(
            dimension_semantics=("arbitrary", "arbitrary"), vmem_limit_bytes=VMEM_LIMIT),
        name="dilated",
    )(*operands)


M_INIT = -1e30


def _moba_pair_tables(n_blocks):
    pairs = [(i, j) for i in range(1, n_blocks) for j in range(i)]
    return np.array([p[0] for p in pairs], np.int32), np.array([p[1] for p in pairs], np.int32)


def _moba_kernel(slopes_ref, qblk_ref, kblk_ref, q0_ref, q1_ref, k0_ref, k1_ref, v0_ref, v1_ref, sel_ref,
                 o_ref, acc_ref, m_ref, l_ref, mask_ref, *rings):
    s_ref, p_ref, mloc_ref, stats_ref = (rings[n * MOBA_RING:(n + 1) * MOBA_RING] for n in range(4))
    pair = pl.program_id(0)
    bs = MOBA_BLOCK
    n_blocks = k0_ref.shape[1] // bs
    n_past = qblk_ref.shape[0]
    sel_rows = LANES // N_HEADS_B
    slopes = (slopes_ref[2 * pair], slopes_ref[2 * pair + 1])
    q_refs, k_refs, v_refs = (q0_ref, q1_ref), (k0_ref, k1_ref), (v0_ref, v1_ref)
    sum_row = (HEAD_DIM, 0)

    @pl.when(pl.program_id(1) == 0)
    def _():
        rel = (lax.broadcasted_iota(jnp.int32, (bs, bs), 0)
               - lax.broadcasted_iota(jnp.int32, (bs, bs), 1))
        mask_ref[...] = jnp.where(rel <= 0, 0.0, NEG_INF)

    m_ref[...] = jnp.full_like(m_ref, M_INIT)
    l_ref[...] = jnp.zeros_like(l_ref)
    acc_ref[...] = jnp.zeros_like(acc_ref)
    q_pos = lax.broadcasted_iota(jnp.int32, (1, bs), 1).astype(jnp.float32)

    def make_stages(blocks_of, own):
        def scores(t, slot):
            i, j = blocks_of(t)
            q_off = pl.multiple_of(i * bs, bs)
            k_off = pl.multiple_of(j * bs, bs)
            for h in range(2):
                st = _dot_nt(k_refs[h][0, pl.ds(k_off, bs), :], q_refs[h][0, pl.ds(q_off, bs), :])
                if own:
                    st = st + mask_ref[...]
                s_ref[slot][h] = st
                mloc_ref[slot][h:h + 1, :] = jnp.max(st, axis=0, keepdims=True)

        def softmax(t, slot):
            for h in range(2):
                m_loc = mloc_ref[slot][h:h + 1, :]
                stats_ref[slot][h:h + 1, :] = m_loc
                p_ref[slot][h] = jnp.exp2(s_ref[slot][h] - m_loc).astype(p_ref[slot].dtype)

        def merge(t, slot):
            i, j = blocks_of(t)
            q_off = pl.multiple_of(i * bs, bs)
            k_off = pl.multiple_of(j * bs, bs)
            gap = ((i - j) * bs).astype(jnp.float32)
            for h in range(2):
                pv = lax.dot_general(v_refs[h][0, pl.ds(k_off, bs), :], p_ref[slot][h],
                                     (((0,), (0,)), ((), ())), preferred_element_type=jnp.float32)
                out_t = pv[h * HEAD_DIM:(h + 1) * HEAD_DIM]
                l_blk = pv[sum_row[h]:sum_row[h] + 1]
                m_blk = stats_ref[slot][h:h + 1, :] - slopes[h] * (q_pos + gap)
                m_old = m_ref[h:h + 1, pl.ds(q_off, bs)]
                if own:
                    m_new = jnp.maximum(m_old, m_blk)
                    a_blk = jnp.exp2(m_blk - m_new)
                else:
                    chosen = sel_ref[0, pl.ds(h * sel_rows + j, 1), pl.ds(q_off, bs)] > 0.5
                    m_new = jnp.where(chosen, jnp.maximum(m_old, m_blk), m_old)
                    a_blk = jnp.where(chosen, jnp.exp2(m_blk - m_new), 0.0)
                a_old = jnp.exp2(m_old - m_new)
                acc_ref[h, :, pl.ds(q_off, bs)] = a_old * acc_ref[h, :, pl.ds(q_off, bs)] + a_blk * out_t
                l_ref[h:h + 1, pl.ds(q_off, bs)] = a_old * l_ref[h:h + 1, pl.ds(q_off, bs)] + a_blk * l_blk
                m_ref[h:h + 1, pl.ds(q_off, bs)] = m_new

        return scores, softmax, merge

    def own_blocks(t):
        i = jnp.asarray(t, jnp.int32)
        return i, i

    _software_pipeline(n_blocks, *make_stages(own_blocks, own=True), MOBA_OWN_LAG, scores_first=False)
    _software_pipeline(n_past, *make_stages(lambda t: (qblk_ref[t], kblk_ref[t]), own=False),
                       MOBA_LAG, scores_first=False)

    def finish(i, carry):
        q0 = pl.multiple_of(i * bs, bs)
        halves = [acc_ref[h, :, pl.ds(q0, bs)] / l_ref[h:h + 1, pl.ds(q0, bs)] for h in range(2)]
        o_ref[0, pl.ds(q0, bs), :] = jnp.concatenate(halves, axis=0).T.astype(o_ref.dtype)
        return carry

    lax.fori_loop(0, n_blocks, finish, 0, unroll=4)


def _moba(heads, sel, slopes):
    b, s, _ = heads.shape
    n_pairs = WIDTH_B // LANES
    sel_rows = 2 * (LANES // N_HEADS_B)
    qblk, kblk = _moba_pair_tables(s // MOBA_BLOCK)
    smem = pl.BlockSpec(memory_space=pltpu.SMEM)
    tile = lambda n: pl.BlockSpec((1, s, LANES), lambda p, i: (i, 0, 6 * p + n))
    return pl.pallas_call(
        _moba_kernel,
        grid=(n_pairs, b),
        in_specs=[smem, smem, smem] + [tile(n) for n in range(6)]
                 + [pl.BlockSpec((1, sel_rows, s), lambda p, i: (i, p, 0))],
        out_specs=pl.BlockSpec((1, s, LANES), lambda p, i: (i, 0, p)),
        out_shape=jax.ShapeDtypeStruct((b, s, WIDTH_B), jnp.float32),
        scratch_shapes=[pltpu.VMEM((2, HEAD_DIM, s), jnp.float32),
                        pltpu.VMEM((8, s), jnp.float32),
                        pltpu.VMEM((8, s), jnp.float32),
                        pltpu.VMEM((MOBA_BLOCK, MOBA_BLOCK), jnp.float32)]
                       + [pltpu.VMEM((2, MOBA_BLOCK, MOBA_BLOCK), jnp.float32)] * MOBA_RING
                       + [pltpu.VMEM((2, MOBA_BLOCK, MOBA_BLOCK), jnp.bfloat16)] * MOBA_RING
                       + [pltpu.VMEM((8, MOBA_BLOCK), jnp.float32)] * (2 * MOBA_RING),
        compiler_params=pltpu.CompilerParams(
            dimension_semantics=("arbitrary", "arbitrary"), vmem_limit_bytes=VMEM_LIMIT),
        name="moba",
    )(slopes, jnp.asarray(qblk), jnp.asarray(kblk), *([heads] * 6), sel)


def _rms(x, g):
    return (x * lax.rsqrt(jnp.mean(x * x, axis=-1, keepdims=True) + EPS)) * g


def _tail_kernel(x_ref, p_ref, oa_ref, ob_ref, ga_ref, gb_ref, wout_ref, gmlp_ref, wup_ref, wdown_ref,
                 gple_ref, wgate_ref, bgate_ref, wproj_ref, gfin_ref, o_ref):
    bf = jnp.bfloat16
    ya = _rms(oa_ref[0].astype(jnp.float32), ga_ref[...]).astype(bf)
    yb = _rms(ob_ref[0].astype(jnp.float32), gb_ref[...]).astype(bf)
    y = (jnp.dot(ya, wout_ref[0:WIDTH_A, :], preferred_element_type=jnp.float32)
         + jnp.dot(yb, wout_ref[WIDTH_A:, :], preferred_element_type=jnp.float32))
    h = x_ref[0] + y

    hn = _rms(h, gmlp_ref[...]).astype(bf)
    ff_chunk = 1024
    for c in range(D_FF // ff_chunk):
        u = jnp.dot(hn, wup_ref[:, c * ff_chunk:(c + 1) * ff_chunk], preferred_element_type=jnp.float32)
        u = jnp.square(jnp.maximum(u, 0.0)).astype(bf)
        h = h + jnp.dot(u, wdown_ref[c * ff_chunk:(c + 1) * ff_chunk, :], preferred_element_type=jnp.float32)

    z = jnp.dot(_rms(h, gple_ref[...]).astype(bf), wgate_ref[...], preferred_element_type=jnp.float32)
    gate = jax.nn.sigmoid(z + bgate_ref[...])
    h = h + gate * jnp.dot(p_ref[0].astype(bf), wproj_ref[...], preferred_element_type=jnp.float32)
    o_ref[0] = _rms(h, gfin_ref[...]).astype(o_ref.dtype)


def _tail(x, p, oa, ob, ga, gb, wout, gmlp, wup, wdown, gple, wgate, bgate, wproj, gfin):
    b, s, d = x.shape
    rows = lambda width: pl.BlockSpec((1, TAIL_ROWS, width), lambda i, t: (i, t, 0))
    full = lambda arr: pl.BlockSpec(arr.shape, lambda i, t: (0,) * arr.ndim,
                                    pipeline_mode=pl.Buffered(1))
    consts = (ga, gb, wout, gmlp, wup, wdown, gple, wgate, bgate, wproj, gfin)
    return pl.pallas_call(
        _tail_kernel,
        grid=(b, s // TAIL_ROWS),
        in_specs=[rows(d), rows(PLE_DIM), rows(WIDTH_A), rows(WIDTH_B)] + [full(a) for a in consts],
        out_specs=rows(d),
        out_shape=jax.ShapeDtypeStruct((b, s, d), x.dtype),
        compiler_params=pltpu.CompilerParams(
            dimension_semantics=("arbitrary", "arbitrary"), vmem_limit_bytes=VMEM_LIMIT),
        name="tail",
    )(x, p, oa, ob, *consts)


def kernel(x, p, g_attn, w_in, g_out_a, g_out_b, w_out, g_mlp, w_up, w_down, g_ple, w_ple_gate,
           b_ple_gate, w_ple_proj, g_final):
    assert x.shape[1:] == (4096, D_MODEL) and p.shape[0] == 1 and w_in.shape[0] == 1
    b, s, _ = x.shape
    bf = jnp.bfloat16
    slopes_a, slopes_b = _alibi_log2_slopes()
    row = lambda v: v.reshape(1, -1)

    qkv_a, gate, qkv_a4, qkv_a16, moba_heads = _project(x, row(g_attn[0]), w_in[0].astype(bf))
    sel = _select(gate)
    oa = _dilated([qkv_a, qkv_a4.reshape(b, s, 3 * WIDTH_A), qkv_a16.reshape(b, s, 3 * WIDTH_A)],
                  jnp.asarray(slopes_a))
    ob = _moba(moba_heads, sel, jnp.asarray(slopes_b))

    return _tail(x, p[0], oa, ob, row(g_out_a[0]), row(g_out_b[0]), w_out[0].astype(bf), row(g_mlp[0]),
                 w_up[0].astype(bf), w_down[0].astype(bf), row(g_ple[0]), w_ple_gate[0].astype(bf),
                 row(b_ple_gate[0]), w_ple_proj[0].astype(bf), row(g_final))
```

```python
import functools

import jax
import jax.numpy as jnp
import numpy as np
from jax import lax
from jax.experimental import pallas as pl
from jax.experimental.pallas import tpu as pltpu

D_MODEL = 1024
HEAD_DIM = 64
N_HEADS = D_MODEL // HEAD_DIM
N_HEADS_A = N_HEADS // 2
N_HEADS_B = N_HEADS - N_HEADS_A
WIDTH_A = N_HEADS_A * HEAD_DIM
WIDTH_B = N_HEADS_B * HEAD_DIM
DILATED_PAIRS = ((128, 1), (512, 4), (2048, 16))
MOBA_BLOCK = 256
MOBA_TOPK = 3
D_FF = 4 * D_MODEL
PLE_DIM = 256
EPS = 1e-6

LANES = 128
LOG2E = 1.4426950408889634
Q_SCALE = HEAD_DIM ** -0.5 * LOG2E
NEG_INF = float("-inf")
VMEM_V7X_BYTES = 64 * 1024 * 1024
VMEM_LIMIT = VMEM_V7X_BYTES * 7 // 8

PROJ_ROWS = 512
TAIL_ROWS = 512
DIL_BLOCK = 128


def _alibi_log2_slopes():
    s = np.array([2.0 ** (-8.0 * (i + 1) / N_HEADS) for i in range(N_HEADS)], dtype=np.float64)
    return (s[0::2] * LOG2E).astype(np.float32), (s[1::2] * LOG2E).astype(np.float32)


def _dot_nt(a, b):
    return lax.dot_general(a, b, (((1,), (1,)), ((), ())), preferred_element_type=jnp.float32)


def _split_bf16(x):
    hi = x.astype(jnp.bfloat16)
    lo = (x - hi.astype(jnp.float32)).astype(jnp.bfloat16)
    return hi, lo


DIL_LAG = 4
MOBA_OWN_LAG = 4
MOBA_LAG = 15
MOBA_RING = 2 * MOBA_LAG


def _software_pipeline(n_steps, scores, softmax, emit, lag, scores_first=True):
    ring = 2 * lag
    assert n_steps % ring == 0
    for t in range(ring):
        scores(t, t)
    for t in range(lag):
        softmax(t, t)

    def body(it, carry, last=False):
        for k in range(ring):
            t = ring * it + k
            if scores_first and not last:
                scores(t + ring, k)
            if not last or k < lag:
                softmax(t + lag, (k + lag) % ring)
            emit(t, k)
            if not scores_first and not last:
                scores(t + ring, k)
        return carry

    lax.fori_loop(0, n_steps // ring - 1, body, 0)
    body(n_steps // ring - 1, 0, last=True)


def _moba_slope_split():
    _, slopes = _alibi_log2_slopes()
    hi = slopes.astype(jnp.bfloat16).astype(np.float32)
    lo = (slopes - hi).astype(jnp.bfloat16).astype(np.float32)
    return hi, lo


def _proj_kernel(x_ref, g_ref, w_ref, qkv_ref, gate_ref, a4_ref, a16_ref, moba_ref, kmean_ref, slab_ref,
                 *, slope_hi, slope_lo):
    t = pl.program_id(1)
    x = x_ref[0]
    ms = jnp.mean(x * x, axis=-1, keepdims=True)
    hn = (x * lax.rsqrt(ms + EPS)) * g_ref[...]
    qkv = jnp.dot(hn.astype(jnp.bfloat16), w_ref[...], preferred_element_type=jnp.float32)

    qa = qkv[:, 0:WIDTH_A]
    qb = qkv[:, 3 * WIDTH_A:3 * WIDTH_A + WIDTH_B]
    kb = qkv[:, 3 * WIDTH_A + WIDTH_B:3 * WIDTH_A + 2 * WIDTH_B]
    vb = qkv[:, 3 * WIDTH_A + 2 * WIDTH_B:]
    qkv_ref[0, :, 0:WIDTH_A] = (qa * Q_SCALE).astype(qkv_ref.dtype)
    qkv_ref[0, :, WIDTH_A:] = qkv[:, WIDTH_A:3 * WIDTH_A].astype(qkv_ref.dtype)

    lane = lax.broadcasted_iota(jnp.int32, (PROJ_ROWS, LANES), 1)
    key_pos = jnp.bitwise_and(lax.broadcasted_iota(jnp.int32, (PROJ_ROWS, LANES), 0),
                              MOBA_BLOCK - 1).astype(jnp.float32)
    for pair in range(WIDTH_B // LANES):
        cols = slice(pair * LANES, (pair + 1) * LANES)
        q_t, k_t, v_t = qb[:, cols] * Q_SCALE, kb[:, cols], vb[:, cols]
        for h in range(2):
            own = (lane < HEAD_DIM) if h == 0 else (lane >= HEAD_DIM)
            f0 = HEAD_DIM if h == 0 else 0
            head = 2 * pair + h
            q_feat = jnp.where(lane == f0, float(slope_hi[head]),
                               jnp.where(lane == f0 + 1, float(slope_lo[head]), 0.0))
            k_feat = jnp.where((lane == f0) | (lane == f0 + 1), key_pos, 0.0)
            base = (pair * 6 + h) * LANES
            for n, tile in enumerate((jnp.where(own, q_t, q_feat), jnp.where(own, k_t, k_feat),
                                      jnp.where(own, v_t, 1.0))):
                moba_ref[0, :, base + 2 * n * LANES:base + (2 * n + 1) * LANES] = tile.astype(moba_ref.dtype)

    rows4 = PROJ_ROWS // 4
    rows16 = PROJ_ROWS // 16
    for cb in range(3 * WIDTH_A // LANES):
        cols = slice(cb * LANES, (cb + 1) * LANES)
        val = qkv[:, cols]
        nat, res4 = 2 * (cb % 2), 2 * (cb % 2) + 1
        slab_ref[nat] = val * Q_SCALE if cb < WIDTH_A // LANES else val
        for r4 in range(4):
            part = slab_ref[nat, pl.ds(r4, rows4, stride=4), :]
            a4_ref[0, r4, :, cols] = part.astype(a4_ref.dtype)
            slab_ref[res4, r4 * rows4:(r4 + 1) * rows4, :] = part
        for r4 in range(4):
            for j in range(4):
                part = slab_ref[res4, pl.ds(r4 * rows4 + j, rows16, stride=4), :]
                a16_ref[0, 4 * j + r4, :, cols] = part.astype(a16_ref.dtype)

    @pl.when(t == 0)
    def _():
        kmean_ref[...] = jnp.zeros_like(kmean_ref)

    blocks_per_tile = PROJ_ROWS // MOBA_BLOCK
    for sb in range(blocks_per_tile):
        km = jnp.mean(kb[sb * MOBA_BLOCK:(sb + 1) * MOBA_BLOCK, :], axis=0, keepdims=True)
        kmean_ref[pl.ds(t * blocks_per_tile + sb, 1), :] = km

    kmean = kmean_ref[...]
    lane_head = lax.broadcasted_iota(jnp.int32, kmean.shape, 1) // HEAD_DIM
    wt = jnp.concatenate(
        [jnp.where(lane_head == h, kmean, 0.0) for h in range(N_HEADS_B)], axis=0)
    q_hi, q_lo = _split_bf16(qb)
    w_hi, w_lo = _split_bf16(wt)
    gate_ref[0] = _dot_nt(w_hi, q_hi) + (_dot_nt(w_lo, q_hi) + _dot_nt(w_hi, q_lo))


def _project(x, g_attn, w_in_bf16):
    b, s, d = x.shape
    n_blocks = s // MOBA_BLOCK
    slope_hi, slope_lo = _moba_slope_split()
    return pl.pallas_call(
        functools.partial(_proj_kernel, slope_hi=slope_hi, slope_lo=slope_lo),
        grid=(b, s // PROJ_ROWS),
        in_specs=[
            pl.BlockSpec((1, PROJ_ROWS, d), lambda i, t: (i, t, 0)),
            pl.BlockSpec((1, d), lambda i, t: (0, 0)),
            pl.BlockSpec((d, 3 * d), lambda i, t: (0, 0)),
        ],
        out_specs=[
            pl.BlockSpec((1, PROJ_ROWS, 3 * WIDTH_A), lambda i, t: (i, t, 0)),
            pl.BlockSpec((1, LANES, PROJ_ROWS), lambda i, t: (i, 0, t)),
            pl.BlockSpec((1, 4, PROJ_ROWS // 4, 3 * WIDTH_A), lambda i, t: (i, 0, t, 0)),
            pl.BlockSpec((1, 16, PROJ_ROWS // 16, 3 * WIDTH_A), lambda i, t: (i, 0, t, 0)),
            pl.BlockSpec((1, PROJ_ROWS, 6 * WIDTH_B), lambda i, t: (i, t, 0)),
        ],
        out_shape=[
            jax.ShapeDtypeStruct((b, s, 3 * WIDTH_A), jnp.bfloat16),
            jax.ShapeDtypeStruct((b, LANES, s), jnp.float32),
            jax.ShapeDtypeStruct((b, 4, s // 4, 3 * WIDTH_A), jnp.bfloat16),
            jax.ShapeDtypeStruct((b, 16, s // 16, 3 * WIDTH_A), jnp.bfloat16),
            jax.ShapeDtypeStruct((b, s, 6 * WIDTH_B), jnp.bfloat16),
        ],
        scratch_shapes=[pltpu.VMEM((n_blocks, WIDTH_B), jnp.float32),
                        pltpu.VMEM((4, PROJ_ROWS, LANES), jnp.float32)],
        compiler_params=pltpu.CompilerParams(
            dimension_semantics=("arbitrary", "arbitrary"), vmem_limit_bytes=VMEM_LIMIT),
        name="proj",
    )(x, g_attn, w_in_bf16)


def _select_kernel(gate_ref, sel_ref):
    n_blocks = LANES // N_HEADS_B
    width = gate_ref.shape[2]
    q_pos = pl.program_id(1) * width + lax.broadcasted_iota(jnp.int32, (n_blocks, width), 1)
    blk = lax.broadcasted_iota(jnp.int32, (n_blocks, width), 0)
    past = blk < q_pos // MOBA_BLOCK
    for h in range(N_HEADS_B):
        g = jnp.where(past, gate_ref[0, h * n_blocks:(h + 1) * n_blocks, :], NEG_INF)
        picked = jnp.zeros(g.shape, jnp.bool_)
        for _ in range(MOBA_TOPK):
            best = jnp.max(g, axis=0, keepdims=True)
            first = jnp.min(jnp.where(g == best, blk, n_blocks), axis=0, keepdims=True)
            hit = blk == first
            picked = picked | hit
            g = jnp.where(hit, NEG_INF, g)
        sel_ref[0, h * n_blocks:(h + 1) * n_blocks, :] = (past & picked).astype(sel_ref.dtype)


def _select(gate):
    b, _, s = gate.shape
    width = 2048
    return pl.pallas_call(
        _select_kernel,
        grid=(b, s // width),
        in_specs=[pl.BlockSpec((1, LANES, width), lambda i, t: (i, 0, t))],
        out_specs=pl.BlockSpec((1, LANES, width), lambda i, t: (i, 0, t)),
        out_shape=jax.ShapeDtypeStruct((b, LANES, s), jnp.float32),
        compiler_params=pltpu.CompilerParams(dimension_semantics=("arbitrary", "arbitrary")),
        name="select",
    )(gate)


def _dilated_kernel(slopes_ref, q1_ref, k1_ref, v1_ref, q4_ref, k4_ref, v4_ref,
                    q16_ref, k16_ref, v16_ref, o_ref,
                    acc_ref, stat_ref, bias_ref, s_ref, p_ref, mloc_ref, stats_ref):
    pair = pl.program_id(0)
    seq = k1_ref.shape[1]
    c = DIL_BLOCK
    n_steps = seq // c
    dilations = tuple(d for _, d in DILATED_PAIRS)

    @pl.when(pl.program_id(1) == 0)
    def _():
        key = lax.broadcasted_iota(jnp.int32, (2 * c, 2 * c), 0)
        lane = lax.broadcasted_iota(jnp.int32, (2 * c, 2 * c), 1)
        slope_l = jnp.where(lane >= c, slopes_ref[2 * pair + 1], slopes_ref[2 * pair])
        for b_idx, dil in enumerate(dilations):
            for first in range(2):
                dist = (0 if first else c) + jnp.bitwise_and(lane, c - 1) - key
                bias_ref[2 * b_idx + first] = jnp.where(
                    (dist >= 0) & (dist <= c), (-dil) * slope_l * dist.astype(jnp.float32), NEG_INF)

    head0 = lax.broadcasted_iota(jnp.int32, (c, LANES), 1) < HEAD_DIM

    branches = ((q1_ref, k1_ref, v1_ref), (q4_ref, k4_ref, v4_ref), (q16_ref, k16_ref, v16_ref))
    for b_idx, (dil, (q_ref, k_ref, v_ref)) in enumerate(zip(dilations, branches)):
        sub_len = seq // dil
        n_chunks = sub_len // c

        def offsets(t, n_chunks=n_chunks, sub_len=sub_len):
            t = jnp.asarray(t, jnp.int32)
            r = t // n_chunks
            n = t % n_chunks
            q_off = pl.multiple_of(r * sub_len + n * c, c)
            k_off = pl.multiple_of(r * sub_len + jnp.maximum(n - 1, 0) * c, c)
            return r, n, q_off, k_off

        def scores(t, slot, b_idx=b_idx, q_ref=q_ref, k_ref=k_ref, offsets=offsets):
            _, n, q_off, k_off = offsets(t)
            q = q_ref[0, pl.ds(q_off, c), :]
            q_heads = jnp.concatenate([jnp.where(head0, q, jnp.zeros_like(q)),
                                       jnp.where(head0, jnp.zeros_like(q), q)], axis=0)
            keys = k_ref[0, pl.ds(k_off, 2 * c), :]
            first = (n == 0).astype(jnp.int32)
            st = _dot_nt(keys, q_heads) + bias_ref[2 * b_idx + first]
            s_ref[slot] = st
            mloc_ref[slot, 0:1, :] = jnp.max(st, axis=0, keepdims=True)

        def softmax(t, slot):
            m_loc = mloc_ref[slot, 0:1, :]
            p = jnp.exp2(s_ref[slot] - m_loc)
            stats_ref[slot, 0:1, :] = m_loc
            stats_ref[slot, 1:2, :] = jnp.sum(p, axis=0, keepdims=True)
            p_ref[slot] = p.astype(p_ref.dtype)

        def emit(t, slot, b_idx=b_idx, dil=dil, v_ref=v_ref, offsets=offsets):
            r, n, _, k_off = offsets(t)
            vals = v_ref[0, pl.ds(k_off, 2 * c), :]
            pv = lax.dot_general(vals, p_ref[slot], (((0,), (0,)), ((), ())),
                                 preferred_element_type=jnp.float32)
            out_t = jnp.concatenate([pv[:HEAD_DIM, :c], pv[HEAD_DIM:, c:]], axis=0)
            if dil == 1:
                rows = pl.ds(pl.multiple_of(n * c, c), c)
            else:
                rows = pl.ds(r + dil * c * n, c, stride=dil)
            l_loc = stats_ref[slot, 1:2, :]
            inv_l = 1.0 / l_loc
            lse = stats_ref[slot, 0:1, :] + jnp.log2(l_loc)
            out_t = out_t * jnp.concatenate([jnp.broadcast_to(inv_l[:, :c], (HEAD_DIM, c)),
                                             jnp.broadcast_to(inv_l[:, c:], (HEAD_DIM, c))], axis=0)
            lse_t = jnp.concatenate([jnp.broadcast_to(lse[:, :c], (HEAD_DIM, c)),
                                     jnp.broadcast_to(lse[:, c:], (HEAD_DIM, c))], axis=0)
            acc_ref[b_idx, rows, :] = out_t.T
            stat_ref[b_idx, rows, :] = lse_t.T

        _software_pipeline(n_steps, scores, softmax, emit, DIL_LAG, scores_first=False)

    def merge(ci, carry):
        rows = pl.ds(pl.multiple_of(ci * c, c), c)
        lses = [stat_ref[b_idx, rows, :] for b_idx in range(len(dilations))]
        lse_max = functools.reduce(jnp.maximum, lses)
        weights = [jnp.exp2(lse - lse_max) for lse in lses]
        out = functools.reduce(lambda a, b: a + b,
                               [w * acc_ref[b_idx, rows, :] for b_idx, w in enumerate(weights)])
        o_ref[0, rows, :] = (out / functools.reduce(lambda a, b: a + b, weights)).astype(o_ref.dtype)
        return carry

    lax.fori_loop(0, seq // c, merge, 0, unroll=4)


def _dilated(qkv_by_dilation, slopes):
    b, s, _ = qkv_by_dilation[0].shape
    n_pairs = WIDTH_A // LANES
    tok_major = pl.BlockSpec((1, s, LANES), lambda p, i: (i, 0, p))
    operands, in_specs = [slopes], [pl.BlockSpec(memory_space=pltpu.SMEM)]
    for arr in qkv_by_dilation:
        for part in range(3):
            operands.append(arr)
            in_specs.append(pl.BlockSpec((1, s, LANES), lambda p, i, part=part: (i, 0, part * n_pairs + p)))
    n_br = len(DILATED_PAIRS)
    c = DIL_BLOCK
    return pl.pallas_call(
        _dilated_kernel,
        grid=(n_pairs, b),
        in_specs=in_specs,
        out_specs=tok_major,
        out_shape=jax.ShapeDtypeStruct((b, s, WIDTH_A), jnp.float32),
        scratch_shapes=[pltpu.VMEM((n_br, s, LANES), jnp.float32),
                        pltpu.VMEM((n_br, s, LANES), jnp.float32),
                        pltpu.VMEM((2 * n_br, 2 * c, 2 * c), jnp.float32),
                        pltpu.VMEM((2 * DIL_LAG, 2 * c, 2 * c), jnp.float32),
                        pltpu.VMEM((2 * DIL_LAG, 2 * c, 2 * c), jnp.bfloat16),
                        pltpu.VMEM((2 * DIL_LAG, 8, 2 * c), jnp.float32),
                        pltpu.VMEM((2 * DIL_LAG, 8, 2 * c), jnp.float32)],
        compiler_params=pltpu.CompilerParams(
            dimension_semantics=("arbitrary", "arbitrary"), vmem_limit_bytes=VMEM_LIMIT),
        name="dilated",
    )(*operands)


M_INIT = -1e30


def _moba_pair_tables(n_blocks):
    pairs = [(i, j) for i in range(1, n_blocks) for j in range(i)]
    return np.array([p[0] for p in pairs], np.int32), np.array([p[1] for p in pairs], np.int32)


def _moba_kernel(slopes_ref, qblk_ref, kblk_ref, q0_ref, q1_ref, k0_ref, k1_ref, v0_ref, v1_ref, sel_ref,
                 o_ref, acc_ref, m_ref, l_ref, mask_ref, *rings):
    s_ref, p_ref, mloc_ref, stats_ref = (rings[n * MOBA_RING:(n + 1) * MOBA_RING] for n in range(4))
    pair = pl.program_id(0)
    bs = MOBA_BLOCK
    n_blocks = k0_ref.shape[1] // bs
    n_past = qblk_ref.shape[0]
    sel_rows = LANES // N_HEADS_B
    slopes = (slopes_ref[2 * pair], slopes_ref[2 * pair + 1])
    q_refs, k_refs, v_refs = (q0_ref, q1_ref), (k0_ref, k1_ref), (v0_ref, v1_ref)
    sum_row = (HEAD_DIM, 0)

    @pl.when(pl.program_id(1) == 0)
    def _():
        rel = (lax.broadcasted_iota(jnp.int32, (bs, bs), 0)
               - lax.broadcasted_iota(jnp.int32, (bs, bs), 1))
        mask_ref[...] = jnp.where(rel <= 0, 0.0, NEG_INF)

    m_ref[...] = jnp.full_like(m_ref, M_INIT)
    l_ref[...] = jnp.zeros_like(l_ref)
    acc_ref[...] = jnp.zeros_like(acc_ref)
    q_pos = lax.broadcasted_iota(jnp.int32, (1, bs), 1).astype(jnp.float32)

    def make_stages(blocks_of, own):
        def scores(t, slot):
            i, j = blocks_of(t)
            q_off = pl.multiple_of(i * bs, bs)
            k_off = pl.multiple_of(j * bs, bs)
            for h in range(2):
                st = _dot_nt(k_refs[h][0, pl.ds(k_off, bs), :], q_refs[h][0, pl.ds(q_off, bs), :])
                if own:
                    st = st + mask_ref[...]
                s_ref[slot][h] = st
                mloc_ref[slot][h:h + 1, :] = jnp.max(st, axis=0, keepdims=True)

        def softmax(t, slot):
            for h in range(2):
                m_loc = mloc_ref[slot][h:h + 1, :]
                stats_ref[slot][h:h + 1, :] = m_loc
                p_ref[slot][h] = jnp.exp2(s_ref[slot][h] - m_loc).astype(p_ref[slot].dtype)

        def merge(t, slot):
            i, j = blocks_of(t)
            q_off = pl.multiple_of(i * bs, bs)
            k_off = pl.multiple_of(j * bs, bs)
            gap = ((i - j) * bs).astype(jnp.float32)
            for h in range(2):
                pv = lax.dot_general(v_refs[h][0, pl.ds(k_off, bs), :], p_ref[slot][h],
                                     (((0,), (0,)), ((), ())), preferred_element_type=jnp.float32)
                out_t = pv[h * HEAD_DIM:(h + 1) * HEAD_DIM]
                l_blk = pv[sum_row[h]:sum_row[h] + 1]
                m_blk = stats_ref[slot][h:h + 1, :] - slopes[h] * (q_pos + gap)
                m_old = m_ref[h:h + 1, pl.ds(q_off, bs)]
                if own:
                    m_new = jnp.maximum(m_old, m_blk)
                    a_blk = jnp.exp2(m_blk - m_new)
                else:
                    chosen = sel_ref[0, pl.ds(h * sel_rows + j, 1), pl.ds(q_off, bs)] > 0.5
                    m_new = jnp.where(chosen, jnp.maximum(m_old, m_blk), m_old)
                    a_blk = jnp.where(chosen, jnp.exp2(m_blk - m_new), 0.0)
                a_old = jnp.exp2(m_old - m_new)
                acc_ref[h, :, pl.ds(q_off, bs)] = a_old * acc_ref[h, :, pl.ds(q_off, bs)] + a_blk * out_t
                l_ref[h:h + 1, pl.ds(q_off, bs)] = a_old * l_ref[h:h + 1, pl.ds(q_off, bs)] + a_blk * l_blk
                m_ref[h:h + 1, pl.ds(q_off, bs)] = m_new

        return scores, softmax, merge

    def own_blocks(t):
        i = jnp.asarray(t, jnp.int32)
        return i, i

    _software_pipeline(n_blocks, *make_stages(own_blocks, own=True), MOBA_OWN_LAG, scores_first=False)
    _software_pipeline(n_past, *make_stages(lambda t: (qblk_ref[t], kblk_ref[t]), own=False),
                       MOBA_LAG, scores_first=False)

    def finish(i, carry):
        q0 = pl.multiple_of(i * bs, bs)
        halves = [acc_ref[h, :, pl.ds(q0, bs)] / l_ref[h:h + 1, pl.ds(q0, bs)] for h in range(2)]
        o_ref[0, pl.ds(q0, bs), :] = jnp.concatenate(halves, axis=0).T.astype(o_ref.dtype)
        return carry

    lax.fori_loop(0, n_blocks, finish, 0, unroll=4)


def _moba(heads, sel, slopes):
    b, s, _ = heads.shape
    n_pairs = WIDTH_B // LANES
    sel_rows = 2 * (LANES // N_HEADS_B)
    qblk, kblk = _moba_pair_tables(s // MOBA_BLOCK)
    smem = pl.BlockSpec(memory_space=pltpu.SMEM)
    tile = lambda n: pl.BlockSpec((1, s, LANES), lambda p, i: (i, 0, 6 * p + n))
    return pl.pallas_call(
        _moba_kernel,
        grid=(n_pairs, b),
        in_specs=[smem, smem, smem] + [tile(n) for n in range(6)]
                 + [pl.BlockSpec((1, sel_rows, s), lambda p, i: (i, p, 0))],
        out_specs=pl.BlockSpec((1, s, LANES), lambda p, i: (i, 0, p)),
        out_shape=jax.ShapeDtypeStruct((b, s, WIDTH_B), jnp.float32),
        scratch_shapes=[pltpu.VMEM((2, HEAD_DIM, s), jnp.float32),
                        pltpu.VMEM((8, s), jnp.float32),
                        pltpu.VMEM((8, s), jnp.float32),
                        pltpu.VMEM((MOBA_BLOCK, MOBA_BLOCK), jnp.float32)]
                       + [pltpu.VMEM((2, MOBA_BLOCK, MOBA_BLOCK), jnp.float32)] * MOBA_RING
                       + [pltpu.VMEM((2, MOBA_BLOCK, MOBA_BLOCK), jnp.bfloat16)] * MOBA_RING
                       + [pltpu.VMEM((8, MOBA_BLOCK), jnp.float32)] * (2 * MOBA_RING),
        compiler_params=pltpu.CompilerParams(
            dimension_semantics=("arbitrary", "arbitrary"), vmem_limit_bytes=VMEM_LIMIT),
        name="moba",
    )(slopes, jnp.asarray(qblk), jnp.asarray(kblk), *([heads] * 6), sel)


def _rms(x, g):
    return (x * lax.rsqrt(jnp.mean(x * x, axis=-1, keepdims=True) + EPS)) * g


def _tail_kernel(x_ref, p_ref, oa_ref, ob_ref, ga_ref, gb_ref, wout_ref, gmlp_ref, wup_ref, wdown_ref,
                 gple_ref, wgate_ref, bgate_ref, wproj_ref, gfin_ref, o_ref):
    bf = jnp.bfloat16
    ya = _rms(oa_ref[0].astype(jnp.float32), ga_ref[...]).astype(bf)
    yb = _rms(ob_ref[0].astype(jnp.float32), gb_ref[...]).astype(bf)
    y = (jnp.dot(ya, wout_ref[0:WIDTH_A, :], preferred_element_type=jnp.float32)
         + jnp.dot(yb, wout_ref[WIDTH_A:, :], preferred_element_type=jnp.float32))
    h = x_ref[0] + y

    hn = _rms(h, gmlp_ref[...]).astype(bf)
    ff_chunk = 1024
    for c in range(D_FF // ff_chunk):
        u = jnp.dot(hn, wup_ref[:, c * ff_chunk:(c + 1) * ff_chunk], preferred_element_type=jnp.float32)
        u = jnp.square(jnp.maximum(u, 0.0)).astype(bf)
        h = h + jnp.dot(u, wdown_ref[c * ff_chunk:(c + 1) * ff_chunk, :], preferred_element_type=jnp.float32)

    z = jnp.dot(_rms(h, gple_ref[...]).astype(bf), wgate_ref[...], preferred_element_type=jnp.float32)
    gate = jax.nn.sigmoid(z + bgate_ref[...])
    h = h + gate * jnp.dot(p_ref[0].astype(bf), wproj_ref[...], preferred_element_type=jnp.float32)
    o_ref[0] = _rms(h, gfin_ref[...]).astype(o_ref.dtype)


def _tail(x, p, oa, ob, ga, gb, wout, gmlp, wup, wdown, gple, wgate, bgate, wproj, gfin):
    b, s, d = x.shape
    rows = lambda width: pl.BlockSpec((1, TAIL_ROWS, width), lambda i, t: (i, t, 0))
    full = lambda arr: pl.BlockSpec(arr.shape, lambda i, t: (0,) * arr.ndim,
                                    pipeline_mode=pl.Buffered(1))
    consts = (ga, gb, wout, gmlp, wup, wdown, gple, wgate, bgate, wproj, gfin)
    return pl.pallas_call(
        _tail_kernel,
        grid=(b, s // TAIL_ROWS),
        in_specs=[rows(d), rows(PLE_DIM), rows(WIDTH_A), rows(WIDTH_B)] + [full(a) for a in consts],
        out_specs=rows(d),
        out_shape=jax.ShapeDtypeStruct((b, s, d), x.dtype),
        compiler_params=pltpu.CompilerParams(
            dimension_semantics=("arbitrary", "arbitrary"), vmem_limit_bytes=VMEM_LIMIT),
        name="tail",
    )(x, p, oa, ob, *consts)


def kernel(x, p, g_attn, w_in, g_out_a, g_out_b, w_out, g_mlp, w_up, w_down, g_ple, w_ple_gate,
           b_ple_gate, w_ple_proj, g_final):
    assert x.shape[1:] == (4096, D_MODEL) and p.shape[0] == 1 and w_in.shape[0] == 1
    b, s, _ = x.shape
    bf = jnp.bfloat16
    slopes_a, slopes_b = _alibi_log2_slopes()
    row = lambda v: v.reshape(1, -1)

    qkv_a, gate, qkv_a4, qkv_a16, moba_heads = _project(x, row(g_attn[0]), w_in[0].astype(bf))
    sel = _select(gate)
    oa = _dilated([qkv_a, qkv_a4.reshape(b, s, 3 * WIDTH_A), qkv_a16.reshape(b, s, 3 * WIDTH_A)],
                  jnp.asarray(slopes_a))
    ob = _moba(moba_heads, sel, jnp.asarray(slopes_b))

    return _tail(x, p[0], oa, ob, row(g_out_a[0]), row(g_out_b[0]), w_out[0].astype(bf), row(g_mlp[0]),
                 w_up[0].astype(bf), w_down[0].astype(bf), row(g_ple[0]), w_ple_gate[0].astype(bf),
                 row(b_ple_gate[0]), w_ple_proj[0].astype(bf), row(g_final))
```
